```python
import math
import jax, jax.numpy as jnp
from jax import lax
import numpy as np

D_MODEL = 2048
BATCH = 1
SEQ = 8192
DEPTH = 4
DEC_BATCH = 32
DEC_SEQ = 32
PAST_LEN = 4096

CHUNK = 64
N_MIXERS = 3
D_FF = 5632
NORM_EPS = 1e-6
CONV_WIDTH = 31
RET_HEADS = 8
RET_DK = D_MODEL // RET_HEADS
RET_DV = 2 * D_MODEL // RET_HEADS
ROPE_BASE = 10000.0
ATT_HEADS = 16
ATT_HD = D_MODEL // ATT_HEADS
N_PREV_CHUNKS = 8
BAND_PAST = N_PREV_CHUNKS * CHUNK
BAND = BAND_PAST + CHUNK
REL_CLIP = 256
N_CONV_LAYERS = (DEPTH + 2) // 3
N_RET_LAYERS = (DEPTH + 1) // 3
N_ATT_LAYERS = DEPTH // 3

kernel_name = 'hybrid_streaming_encoder_step'


def rmsnorm(x, g):
    xf = x.astype(jnp.float32)
    y = xf * lax.rsqrt(jnp.mean(xf * xf, axis=-1, keepdims=True) + NORM_EPS)
    return (y * g.astype(jnp.float32)).astype(x.dtype)


def swiglu(x, w_gu, w_down):
    g, u = jnp.split(x @ w_gu, 2, axis=-1)
    return (jax.nn.silu(g) * u) @ w_down


def rope(x, pos):
    half = x.shape[-1] // 2
    freq = ROPE_BASE ** (-jnp.arange(half, dtype=jnp.float32) / half)
    ang = pos.astype(jnp.float32)[:, None] * freq[None, :]
    cos = jnp.cos(ang)[None, :, None, :]
    sin = jnp.sin(ang)[None, :, None, :]
    xf = x.astype(jnp.float32)
    x1, x2 = xf[..., :half], xf[..., half:]
    return jnp.concatenate([x1 * cos - x2 * sin, x1 * sin + x2 * cos], axis=-1).astype(x.dtype)


def conv_module(h, buf, w_in, w_dw, b_dw, g_norm, w_out):
    a, b = jnp.split(h @ w_in, 2, axis=-1)
    u = a * jax.nn.sigmoid(b)
    ext = jnp.concatenate([buf.astype(u.dtype), u], axis=1)
    y = lax.conv_general_dilated(
        ext, w_dw[:, None, :].astype(u.dtype), window_strides=(1,), padding='VALID',
        dimension_numbers=('NWC', 'WIO', 'NWC'), feature_group_count=D_MODEL) + b_dw
    y = jax.nn.silu(rmsnorm(y, g_norm)) @ w_out
    return y, ext[:, ext.shape[1] - (CONV_WIDTH - 1):]


def ret_chunk(S, q, k, v, log_g):
    L = q.shape[1]
    idx = jnp.arange(L, dtype=jnp.float32)
    diff = idx[:, None] - idx[None, :]
    decay = jnp.where(diff >= 0, jnp.exp(log_g[:, None, None] * jnp.maximum(diff, 0.0)), 0.0)
    inner = jnp.einsum('blhd,bmhd->bhlm', q, k) * decay[None]
    o = jnp.einsum('bhlm,bmhe->blhe', inner, v)
    q_dec = jnp.exp(log_g[None, :] * (idx[:, None] + 1.0))
    o = o + jnp.einsum('blhd,bhde->blhe', q, S) * q_dec[None, :, :, None]
    k_dec = jnp.exp(log_g[None, :] * (L - 1.0 - idx[:, None]))
    S_new = S * jnp.exp(log_g * L)[None, :, None, None] + jnp.einsum(
        'blhd,blhe->bhde', k * k_dec[None, :, :, None], v)
    return S_new, o


def retention_mixer(h, S0, pos, w_in, g_out, w_out):
    B, T, _ = h.shape
    qd, vd = RET_HEADS * RET_DK, RET_HEADS * RET_DV
    q, k, v, gate = jnp.split(h @ w_in, [qd, 2 * qd, 2 * qd + vd], axis=-1)
    q = rope(q.reshape(B, T, RET_HEADS, RET_DK), pos).astype(jnp.float32)
    k = rope(k.reshape(B, T, RET_HEADS, RET_DK), pos).astype(jnp.float32) * RET_DK ** -0.5
    v = v.reshape(B, T, RET_HEADS, RET_DV).astype(jnp.float32)
    log_g = jnp.log1p(-jnp.exp2(-5.0 - jnp.arange(RET_HEADS, dtype=jnp.float32)))
    blk = CHUNK if T % CHUNK == 0 else T
    n = T // blk
    to_chunks = lambda a: jnp.swapaxes(a.reshape(B, n, blk, *a.shape[2:]), 0, 1)
    step = lambda S, qkv: ret_chunk(S, qkv[0], qkv[1], qkv[2], log_g)
    S, o = lax.scan(step, S0.astype(jnp.float32), (to_chunks(q), to_chunks(k), to_chunks(v)))
    o = jnp.swapaxes(o, 0, 1).reshape(B, T, RET_HEADS, RET_DV)
    o = o * lax.rsqrt(jnp.mean(o * o, axis=-1, keepdims=True) + NORM_EPS)
    o = (o.reshape(B, T, vd) * g_out.astype(jnp.float32)).astype(h.dtype)
    return (jax.nn.silu(gate) * o) @ w_out, S.astype(h.dtype)


def att_qkv(h, w_qkv, g_q, g_k):
    B, T, _ = h.shape
    q, k, v = jnp.split(h @ w_qkv, 3, axis=-1)
    q = rmsnorm(q.reshape(B, T, ATT_HEADS, ATT_HD), g_q)
    k = rmsnorm(k.reshape(B, T, ATT_HEADS, ATT_HD), g_k)
    v = v.reshape(B, T, ATT_HEADS, ATT_HD)
    return q, k, v


def band_attention(q, k, v, q_pos, k_pos, k_valid, rel_table):
    s = jnp.einsum('bqhd,bkhd->bhqk', q, k, preferred_element_type=jnp.float32) * ATT_HD ** -0.5
    rel = jnp.clip(q_pos[:, None] - k_pos[None, :], -REL_CLIP, REL_CLIP) + REL_CLIP
    s = s + rel_table[:, rel].astype(jnp.float32)[None]
    if k_valid is not None:
        s = jnp.where(k_valid[None, None, None, :], s, -jnp.inf)
    p = jax.nn.softmax(s, axis=-1).astype(v.dtype)
    return jnp.einsum('bhqk,bkhd->bqhd', p, v)


def att_prompt(h, w_qkv, g_q, g_k, rel_table, w_out):
    B, T, _ = h.shape
    q, k, v = att_qkv(h, w_qkv, g_q, g_k)
    pad = ((0, 0), (BAND_PAST, 0), (0, 0), (0, 0))
    kp, vp = jnp.pad(k, pad), jnp.pad(v, pad)

    def one_chunk(c):
        start = c * CHUNK
        qc = lax.dynamic_slice_in_dim(q, start, CHUNK, axis=1)
        kc = lax.dynamic_slice_in_dim(kp, start, BAND, axis=1)
        vc = lax.dynamic_slice_in_dim(vp, start, BAND, axis=1)
        q_pos = start + jnp.arange(CHUNK, dtype=jnp.int32)
        k_pos = start - BAND_PAST + jnp.arange(BAND, dtype=jnp.int32)
        return band_attention(qc, kc, vc, q_pos, k_pos, k_pos >= 0, rel_table)

    o = lax.map(one_chunk, jnp.arange(T // CHUNK, dtype=jnp.int32))
    o = jnp.swapaxes(o, 0, 1).reshape(B, T, D_MODEL)
    keep = min(BAND_PAST, T)
    return o @ w_out, k[:, T - keep:], v[:, T - keep:]


def att_sample(h, k_past, v_past, past_len, w_qkv, g_q, g_k, rel_table, w_out):
    B, T, _ = h.shape
    q, k, v = att_qkv(h, w_qkv, g_q, g_k)
    Lc = k_past.shape[1]
    kc = jnp.concatenate([k_past.astype(k.dtype), k], axis=1)
    vc = jnp.concatenate([v_past.astype(v.dtype), v], axis=1)
    q_pos = past_len + jnp.arange(T, dtype=jnp.int32)
    k_pos = past_len - Lc + jnp.arange(Lc + T, dtype=jnp.int32)
    o = band_attention(q, kc, vc, q_pos, k_pos, None, rel_table)
    return o.reshape(B, T, D_MODEL) @ w_out, k, v


def setup_inputs(seed: int = 0) -> dict:
    key = jax.random.key(seed)
    ks = jax.random.split(key, 32)
    f32 = jnp.float32
    nrm = lambda k, shape, scale: jax.random.normal(k, shape, f32) * scale
    D, F = D_MODEL, D_FF
    qd, vd = RET_HEADS * RET_DK, RET_HEADS * RET_DV
    att_cache_len = min(BAND_PAST, PAST_LEN)
    return {
        'x_prompt': nrm(ks[0], (BATCH, SEQ, D), 1.0),
        'x_sample': nrm(ks[1], (DEC_BATCH, DEC_SEQ, D), 1.0),
        'cache_conv': nrm(ks[2], (N_CONV_LAYERS, DEC_BATCH, CONV_WIDTH - 1, D), 0.5),
        'state_ret': nrm(ks[3], (N_RET_LAYERS, DEC_BATCH, RET_HEADS, RET_DK, RET_DV), 0.5),
        'cache_att_k': nrm(ks[4], (N_ATT_LAYERS, DEC_BATCH, att_cache_len, ATT_HEADS, ATT_HD), 1.0),
        'cache_att_v': nrm(ks[5], (N_ATT_LAYERS, DEC_BATCH, att_cache_len, ATT_HEADS, ATT_HD), 1.0),
        'g_ffn1': 1.0 + nrm(ks[6], (DEPTH, D), 0.02),
        'w_ffn1_gu': nrm(ks[7], (DEPTH, D, 2 * F), D ** -0.5),
        'w_ffn1_down': nrm(ks[8], (DEPTH, F, D), F ** -0.5),
        'g_mix': 1.0 + nrm(ks[9], (DEPTH, D), 0.02),
        'g_ffn2': 1.0 + nrm(ks[10], (DEPTH, D), 0.02),
        'w_ffn2_gu': nrm(ks[11], (DEPTH, D, 2 * F), D ** -0.5),
        'w_ffn2_down': nrm(ks[12], (DEPTH, F, D), F ** -0.5),
        'w_conv_in': nrm(ks[13], (N_CONV_LAYERS, D, 2 * D), D ** -0.5),
        'w_conv_dw': nrm(ks[14], (N_CONV_LAYERS, CONV_WIDTH, D), CONV_WIDTH ** -0.5),
        'b_conv_dw': nrm(ks[15], (N_CONV_LAYERS, D), 0.02),
        'g_conv_norm': 1.0 + nrm(ks[16], (N_CONV_LAYERS, D), 0.02),
        'w_conv_out': nrm(ks[17], (N_CONV_LAYERS, D, D), D ** -0.5),
        'w_ret_in': nrm(ks[18], (N_RET_LAYERS, D, 2 * qd + 2 * vd), D ** -0.5),
        'g_ret_out': 1.0 + nrm(ks[19], (N_RET_LAYERS, vd), 0.02),
        'w_ret_out': nrm(ks[20], (N_RET_LAYERS, vd, D), vd ** -0.5),
        'w_att_qkv': nrm(ks[21], (N_ATT_LAYERS, D, 3 * D), D ** -0.5),
        'g_att_q': 1.0 + nrm(ks[22], (N_ATT_LAYERS, ATT_HD), 0.02),
        'g_att_k': 1.0 + nrm(ks[23], (N_ATT_LAYERS, ATT_HD), 0.02),
        'att_rel_bias': nrm(ks[24], (N_ATT_LAYERS, ATT_HEADS, 2 * REL_CLIP + 1), 0.1),
        'w_att_out': nrm(ks[25], (N_ATT_LAYERS, D, D), D ** -0.5),
    }


def reference(x_prompt, x_sample, cache_conv, state_ret, cache_att_k, cache_att_v,
              g_ffn1, w_ffn1_gu, w_ffn1_down, g_mix, g_ffn2, w_ffn2_gu, w_ffn2_down,
              w_conv_in, w_conv_dw, b_conv_dw, g_conv_norm, w_conv_out,
              w_ret_in, g_ret_out, w_ret_out,
              w_att_qkv, g_att_q, g_att_k, att_rel_bias, w_att_out):
    Bp, Tp, _ = x_prompt.shape
    Ts = x_sample.shape[1]
    pos_p = jnp.arange(Tp, dtype=jnp.int32)
    pos_s = PAST_LEN + jnp.arange(Ts, dtype=jnp.int32)
    xp, xs = x_prompt, x_sample
    conv_p, conv_s, ret_p, ret_s = [], [], [], []
    kp_l, vp_l, ks_l, vs_l = [], [], [], []
    for i in range(DEPTH):
        kind, j = i % N_MIXERS, i // N_MIXERS
        xp = xp + 0.5 * swiglu(rmsnorm(xp, g_ffn1[i]), w_ffn1_gu[i], w_ffn1_down[i])
        xs = xs + 0.5 * swiglu(rmsnorm(xs, g_ffn1[i]), w_ffn1_gu[i], w_ffn1_down[i])
        hp, hs = rmsnorm(xp, g_mix[i]), rmsnorm(xs, g_mix[i])
        if kind == 0:
            zero_buf = jnp.zeros((Bp, CONV_WIDTH - 1, D_MODEL), xp.dtype)
            yp, bp = conv_module(hp, zero_buf, w_conv_in[j], w_conv_dw[j], b_conv_dw[j], g_conv_norm[j], w_conv_out[j])
            ys, bs = conv_module(hs, cache_conv[j], w_conv_in[j], w_conv_dw[j], b_conv_dw[j], g_conv_norm[j], w_conv_out[j])
            conv_p.append(bp)
            conv_s.append(bs)
        elif kind == 1:
            zero_state = jnp.zeros((Bp, RET_HEADS, RET_DK, RET_DV), jnp.float32)
            yp, sp = retention_mixer(hp, zero_state, pos_p, w_ret_in[j], g_ret_out[j], w_ret_out[j])
            ys, ss = retention_mixer(hs, state_ret[j], pos_s, w_ret_in[j], g_ret_out[j], w_ret_out[j])
            ret_p.append(sp)
            ret_s.append(ss)
        else:
            yp, kpn, vpn = att_prompt(hp, w_att_qkv[j], g_att_q[j], g_att_k[j], att_rel_bias[j], w_att_out[j])
            ys, ksn, vsn = att_sample(hs, cache_att_k[j], cache_att_v[j], PAST_LEN, w_att_qkv[j],
                                      g_att_q[j], g_att_k[j], att_rel_bias[j], w_att_out[j])
            kp_l.append(kpn)
            vp_l.append(vpn)
            ks_l.append(ksn)
            vs_l.append(vsn)
        xp = xp + yp
        xs = xs + ys
        xp = xp + 0.5 * swiglu(rmsnorm(xp, g_ffn2[i]), w_ffn2_gu[i], w_ffn2_down[i])
        xs = xs + 0.5 * swiglu(rmsnorm(xs, g_ffn2[i]), w_ffn2_gu[i], w_ffn2_down[i])
    return (xp, xs,
            jnp.stack(conv_p), jnp.stack(conv_s),
            jnp.stack(ret_p), jnp.stack(ret_s),
            jnp.stack(kp_l), jnp.stack(vp_l),
            jnp.stack(ks_l), jnp.stack(vs_l))
```

```python
import functools

import jax
import jax.numpy as jnp
from jax import lax
from jax.experimental import pallas as pl
from jax.experimental.pallas import tpu as pltpu

F32 = jnp.float32
BF16 = jnp.bfloat16

CHUNK = 64
NORM_EPS = 1e-6
CONV_WIDTH = 31
CONV_HALO = CONV_WIDTH - 1
RET_HEADS = 8
RET_DK = 256
RET_DV = 512
ROPE_BASE = 10000.0
ATT_HEADS = 16
ATT_HD = 128
N_PREV_CHUNKS = 8
BAND_PAST = N_PREV_CHUNKS * CHUNK
REL_CLIP = 256
PAST_LEN = 4096
N_MIXERS = 3

LANES = 128
VMEM_LIMIT_BYTES = 56 * 1024 * 1024
MASK_VALUE = -1e30

FFN_TM = 512
FFN_TF = 512
PROJ_TM = 1024
PROJ_TN = 512
CONV_TT = 128
CONV_ROWS = 32
CONV_COLS = 256
CONV_PAD = 32
RET_L = 256
ATT_QB = 256
ATT_KW = ATT_QB + BAND_PAST


def _params(*sem):
    return pltpu.CompilerParams(dimension_semantics=sem, vmem_limit_bytes=VMEM_LIMIT_BYTES)


def _rmsnorm_rows(x, g):
    ms = jnp.mean(x * x, axis=-1, keepdims=True)
    return x * lax.rsqrt(ms + NORM_EPS) * g


def _silu(x):
    return x * jax.nn.sigmoid(x)


def _ffn_kernel(x_ref, g_ref, wg_ref, wu_ref, wd_ref, o_ref, h_ref):
    j = pl.program_id(1)

    @pl.when(j == 0)
    def _():
        x = x_ref[...]
        h_ref[...] = _rmsnorm_rows(x, g_ref[...]).astype(BF16)
        o_ref[...] = x

    h = h_ref[...]
    gate = jnp.dot(h, wg_ref[...], preferred_element_type=F32)
    up = jnp.dot(h, wu_ref[...], preferred_element_type=F32)
    a = (_silu(gate) * (0.5 * up)).astype(BF16)
    o_ref[...] += jnp.dot(a, wd_ref[...], preferred_element_type=F32)


def _ffn(x, g, w_gu, w_down):
    t, d = x.shape
    f = w_down.shape[0]
    nf = f // FFN_TF
    return pl.pallas_call(
        _ffn_kernel,
        grid=(t // FFN_TM, nf),
        in_specs=[
            pl.BlockSpec((FFN_TM, d), lambda i, j: (i, 0)),
            pl.BlockSpec((1, d), lambda i, j: (0, 0)),
            pl.BlockSpec((d, FFN_TF), lambda i, j: (0, j)),
            pl.BlockSpec((d, FFN_TF), lambda i, j: (0, j + nf)),
            pl.BlockSpec((FFN_TF, d), lambda i, j: (j, 0)),
        ],
        out_specs=pl.BlockSpec((FFN_TM, d), lambda i, j: (i, 0)),
        out_shape=jax.ShapeDtypeStruct((t, d), F32),
        scratch_shapes=[pltpu.VMEM((FFN_TM, d), BF16)],
        compiler_params=_params("parallel", "arbitrary"),
        name="ffn",
    )(x, g.reshape(1, d), w_gu, w_gu, w_down)


def _proj_kernel(*refs, act, rope, k_scale_from, headnorm, n_out):
    it = iter(refs)
    x_ref, g_ref, w_ref = next(it), next(it), next(it)
    cos_ref = sin_ref = gh_ref = None
    if rope:
        cos_ref, sin_ref = next(it), next(it)
    if headnorm:
        gh_ref = next(it)
    out_refs = [next(it) for _ in range(n_out)]
    h_ref = next(it)
    j = pl.program_id(1)

    @pl.when(j == 0)
    def _():
        h_ref[...] = _rmsnorm_rows(x_ref[...], g_ref[...]).astype(BF16)

    y = jnp.dot(h_ref[...], w_ref[...], preferred_element_type=F32)
    tn = y.shape[1]
    if act == "silu":
        y = _silu(y)
    if rope:
        cos, sin = cos_ref[...], sin_ref[...]
        scale = jnp.where(j >= k_scale_from, RET_DK ** -0.5, 1.0).astype(F32)
        half = RET_DK // 2
        parts = []
        for hh in range(tn // RET_DK):
            x1 = y[:, hh * RET_DK: hh * RET_DK + half]
            x2 = y[:, hh * RET_DK + half: (hh + 1) * RET_DK]
            parts.append((x1 * cos - x2 * sin) * scale)
            parts.append((x1 * sin + x2 * cos) * scale)
        y = jnp.concatenate(parts, axis=1)
    if headnorm:
        gh = gh_ref[...]
        parts = []
        for hh in range(tn // ATT_HD):
            parts.append(_rmsnorm_rows(y[:, hh * ATT_HD:(hh + 1) * ATT_HD], gh))
        y = jnp.concatenate(parts, axis=1)
    for o_ref in out_refs:
        o_ref[...] = y.astype(o_ref.dtype)


def _proj(x, g, w, col0, ncols, out_dtypes, *, act=None, rope=None, k_scale_from=0, headnorm=None):
    t, d = x.shape
    tm, tn = PROJ_TM, PROJ_TN
    cb0 = col0 // tn
    in_specs = [
        pl.BlockSpec((tm, d), lambda i, j: (i, 0)),
        pl.BlockSpec((1, d), lambda i, j: (0, 0)),
        pl.BlockSpec((d, tn), lambda i, j: (0, j + cb0)),
    ]
    args = [x, g.reshape(1, d), w]
    if rope is not None:
        half = RET_DK // 2
        in_specs += [pl.BlockSpec((tm, half), lambda i, j: (i, 0))] * 2
        args += list(rope)
    if headnorm is not None:
        in_specs.append(pl.BlockSpec((1, ATT_HD), lambda i, j: (0, 0)))
        args.append(headnorm.reshape(1, ATT_HD))
    kern = functools.partial(_proj_kernel, act=act, rope=rope is not None, k_scale_from=k_scale_from,
                             headnorm=headnorm is not None, n_out=len(out_dtypes))
    outs = pl.pallas_call(
        kern,
        grid=(t // tm, ncols // tn),
        in_specs=in_specs,
        out_specs=[pl.BlockSpec((tm, tn), lambda i, j: (i, j)) for _ in out_dtypes],
        out_shape=[jax.ShapeDtypeStruct((t, ncols), dt) for dt in out_dtypes],
        scratch_shapes=[pltpu.VMEM((tm, d), BF16)],
        compiler_params=_params("parallel", "arbitrary"),
        name="proj",
    )(*args)
    return outs


def _glu_kernel(x_ref, g_ref, wa_ref, wb_ref, o_ref, h_ref):
    @pl.when(pl.program_id(1) == 0)
    def _():
        h_ref[...] = _rmsnorm_rows(x_ref[...], g_ref[...]).astype(BF16)

    h = h_ref[...]
    a = jnp.dot(h, wa_ref[...], preferred_element_type=F32)
    b = jnp.dot(h, wb_ref[...], preferred_element_type=F32)
    o_ref[...] = a * jax.nn.sigmoid(b)


def _glu(x, g, w_in):
    t, d = x.shape
    tm, tn = PROJ_TM, PROJ_TN
    nb = d // tn
    return pl.pallas_call(
        _glu_kernel,
        grid=(t // tm, nb),
        in_specs=[
            pl.BlockSpec((tm, d), lambda i, j: (i, 0)),
            pl.BlockSpec((1, d), lambda i, j: (0, 0)),
            pl.BlockSpec((d, tn), lambda i, j: (0, j)),
            pl.BlockSpec((d, tn), lambda i, j: (0, j + nb)),
        ],
        out_specs=pl.BlockSpec((tm, tn), lambda i, j: (i, j)),
        out_shape=jax.ShapeDtypeStruct((t, d), F32),
        scratch_shapes=[pltpu.VMEM((tm, d), BF16)],
        compiler_params=_params("parallel", "arbitrary"),
        name="conv_glu",
    )(x, g.reshape(1, d), w_in, w_in)


def _out_proj_kernel(x_ref, y_ref, w_ref, o_ref):
    o_ref[...] = x_ref[...] + jnp.dot(y_ref[...], w_ref[...], preferred_element_type=F32)


def _out_proj(x, y, w):
    t, d = x.shape
    k = y.shape[1]
    tm, tn = PROJ_TM, PROJ_TN
    return pl.pallas_call(
        _out_proj_kernel,
        grid=(t // tm, d // tn),
        in_specs=[
            pl.BlockSpec((tm, tn), lambda i, j: (i, j)),
            pl.BlockSpec((tm, k), lambda i, j: (i, 0)),
            pl.BlockSpec((k, tn), lambda i, j: (0, j)),
        ],
        out_specs=pl.BlockSpec((tm, tn), lambda i, j: (i, j)),
        out_shape=jax.ShapeDtypeStruct((t, d), F32),
        compiler_params=_params("parallel", "arbitrary"),
        name="out_proj",
    )(x, y, w)


def _conv_kernel(*refs, tt, carry):
    if carry:
        u_ref, w_ref, b_ref, gn_ref, z_ref, ext_ref, y_ref = refs
        halo_ref = None
    else:
        u_ref, halo_ref, w_ref, b_ref, gn_ref, z_ref, ext_ref, y_ref = refs
    d = u_ref.shape[-1]
    off = CONV_PAD - CONV_HALO

    if carry:
        @pl.when(pl.program_id(0) == 0)
        def _():
            ext_ref[0:CONV_PAD, :] = jnp.zeros((CONV_PAD, d), F32)
    else:
        ext_ref[off:CONV_PAD, :] = halo_ref[...]
    ext_ref[CONV_PAD:CONV_PAD + tt, :] = u_ref[...]

    for r0 in range(0, tt, CONV_ROWS):
        for c0 in range(0, d, CONV_COLS):
            acc = jnp.zeros((CONV_ROWS, CONV_COLS), F32)
            for w in range(CONV_WIDTH):
                rows = pl.ds(off + r0 + w, CONV_ROWS)
                acc = acc + ext_ref[rows, c0:c0 + CONV_COLS] * w_ref[w:w + 1, c0:c0 + CONV_COLS]
            y_ref[r0:r0 + CONV_ROWS, c0:c0 + CONV_COLS] = acc + b_ref[:, c0:c0 + CONV_COLS]

    z_ref[...] = _silu(_rmsnorm_rows(y_ref[...], gn_ref[...])).astype(z_ref.dtype)

    if carry:
        ext_ref[0:CONV_PAD, :] = ext_ref[tt:tt + CONV_PAD, :]


def _conv_prompt(u, n_rows, w_dw, b_dw, g_norm):
    d = u.shape[1]
    tt = CONV_TT
    kern = functools.partial(_conv_kernel, tt=tt, carry=True)
    return pl.pallas_call(
        kern,
        grid=(n_rows // tt,),
        in_specs=[
            pl.BlockSpec((tt, d), lambda i: (i, 0)),
            pl.BlockSpec((CONV_WIDTH, d), lambda i: (0, 0)),
            pl.BlockSpec((1, d), lambda i: (0, 0)),
            pl.BlockSpec((1, d), lambda i: (0, 0)),
        ],
        out_specs=pl.BlockSpec((tt, d), lambda i: (i, 0)),
        out_shape=jax.ShapeDtypeStruct((n_rows, d), BF16),
        scratch_shapes=[pltpu.VMEM((CONV_PAD + tt, d), F32), pltpu.VMEM((tt, d), F32)],
        compiler_params=_params("arbitrary"),
        name="conv_prompt",
    )(u, w_dw, b_dw.reshape(1, d), g_norm.reshape(1, d))


def _conv_sample(u, row0, cache, w_dw, b_dw, g_norm):
    nb, _, d = cache.shape
    tt = (u.shape[0] - row0) // nb
    rb0 = row0 // tt
    kern = functools.partial(_conv_kernel, tt=tt, carry=False)
    return pl.pallas_call(
        kern,
        grid=(nb,),
        in_specs=[
            pl.BlockSpec((tt, d), lambda b: (b + rb0, 0)),
            pl.BlockSpec((None, CONV_HALO, d), lambda b: (b, 0, 0)),
            pl.BlockSpec((CONV_WIDTH, d), lambda b: (0, 0)),
            pl.BlockSpec((1, d), lambda b: (0, 0)),
            pl.BlockSpec((1, d), lambda b: (0, 0)),
        ],
        out_specs=pl.BlockSpec((tt, d), lambda b: (b, 0)),
        out_shape=jax.ShapeDtypeStruct((nb * tt, d), BF16),
        scratch_shapes=[pltpu.VMEM((CONV_PAD + tt, d), F32), pltpu.VMEM((tt, d), F32)],
        compiler_params=_params("parallel"),
        name="conv_sample",
    )(u, cache, w_dw, b_dw.reshape(1, d), g_norm.reshape(1, d))


def _ret_kernel(lg_ref, gl_ref, q_ref, k_ref, v_ref, sg_ref, s0_ref, go_ref, o_ref, sout_ref, s_ref, *, L):
    h = pl.program_id(1)
    c = pl.program_id(2)
    lg = lg_ref[h]
    g_chunk = gl_ref[h]

    @pl.when(c == 0)
    def _():
        s_ref[...] = s0_ref[...]

    q, k, v = q_ref[...], k_ref[...], v_ref[...]
    ri = lax.broadcasted_iota(jnp.int32, (L, L), 0)
    ci = lax.broadcasted_iota(jnp.int32, (L, L), 1)
    diff = (ri - ci).astype(F32)
    decay = jnp.where(diff >= 0, jnp.exp(lg * jnp.maximum(diff, 0.0)), 0.0)
    inner = lax.dot_general(q, k, (((1,), (1,)), ((), ())), preferred_element_type=F32) * decay
    o = jnp.dot(inner.astype(BF16), v, preferred_element_type=F32)

    state = s_ref[...]
    row = lax.broadcasted_iota(jnp.int32, (L, 1), 0).astype(F32)
    q_dec = jnp.exp(lg * (row + 1.0))
    o = o + jnp.dot(q, state.astype(BF16), preferred_element_type=F32) * q_dec
    k_dec = jnp.exp(lg * (L - 1.0 - row))
    kd = (k.astype(F32) * k_dec).astype(BF16)
    new_state = state * g_chunk + lax.dot_general(kd, v, (((0,), (0,)), ((), ())), preferred_element_type=F32)
    s_ref[...] = new_state

    @pl.when(c == pl.num_programs(2) - 1)
    def _():
        sout_ref[...] = new_state

    o = _rmsnorm_rows(o, go_ref[...])
    o_ref[...] = (sg_ref[...].astype(F32) * o).astype(o_ref.dtype)


def _retention(qk, v, sgate, s0, g_out, row0, n_seq, seq_len, L, decay_consts):
    nc = seq_len // L
    rb0 = row0 // L
    h_, dk, dv = RET_HEADS, RET_DK, RET_DV
    row_map = lambda b, h, c: b * nc + c + rb0
    kern = functools.partial(_ret_kernel, L=L)
    lg, gl = decay_consts
    o, s_out = pl.pallas_call(
        kern,
        grid=(n_seq, h_, nc),
        in_specs=[
            pl.BlockSpec(memory_space=pltpu.SMEM),
            pl.BlockSpec(memory_space=pltpu.SMEM),
            pl.BlockSpec((L, dk), lambda b, h, c: (row_map(b, h, c), h)),
            pl.BlockSpec((L, dk), lambda b, h, c: (row_map(b, h, c), h + h_)),
            pl.BlockSpec((L, dv), lambda b, h, c: (row_map(b, h, c), h)),
            pl.BlockSpec((L, dv), lambda b, h, c: (row_map(b, h, c), h)),
            pl.BlockSpec((None, None, dk, dv), lambda b, h, c: (b, h, 0, 0)),
            pl.BlockSpec((1, dv), lambda b, h, c: (0, h)),
        ],
        out_specs=[
            pl.BlockSpec((L, dv), lambda b, h, c: (b * nc + c, h)),
            pl.BlockSpec((None, None, dk, dv), lambda b, h, c: (b, h, 0, 0)),
        ],
        out_shape=[
            jax.ShapeDtypeStruct((n_seq * seq_len, h_ * dv), BF16),
            jax.ShapeDtypeStruct((n_seq, h_, dk, dv), F32),
        ],
        scratch_shapes=[pltpu.VMEM((dk, dv), F32)],
        compiler_params=_params("parallel", "parallel", "arbitrary"),
        name="retention",
    )(lg, gl, qk, qk, v, sgate, s0, g_out.reshape(1, h_ * dv))
    return o, s_out


def _softmax_weights(s):
    m = jnp.max(s, axis=-1, keepdims=True)
    p = jnp.exp(s - m)
    return p, jnp.sum(p, axis=-1, keepdims=True)


def _att_prompt_kernel(q_ref, k0_ref, k1_ref, k2_ref, v0_ref, v1_ref, v2_ref, bias_ref, o_ref):
    qb = pl.program_id(1)
    q = q_ref[...]
    k = jnp.concatenate([k0_ref[...], k1_ref[...], k2_ref[...]], axis=0)
    v = jnp.concatenate([v0_ref[...], v1_ref[...], v2_ref[...]], axis=0)
    s = lax.dot_general(q, k, (((1,), (1,)), ((), ())), preferred_element_type=F32)
    s = s * (ATT_HD ** -0.5) + bias_ref[...]
    k_pos = qb * ATT_QB - BAND_PAST + lax.broadcasted_iota(jnp.int32, s.shape, 1)
    s = jnp.where(k_pos >= 0, s, MASK_VALUE)
    p, l = _softmax_weights(s)
    o = jnp.dot(p.astype(BF16), v, preferred_element_type=F32)
    o_ref[...] = (o / l).astype(o_ref.dtype)


def _att_prompt(q, k, v, bias, n_rows):
    nqb = n_rows // ATT_QB
    nw = ATT_KW // ATT_QB
    blk = (ATT_QB, ATT_HD)

    def kv_spec(back):
        return pl.BlockSpec(blk, lambda h, i: (jnp.maximum(i - back, 0), h))

    kv_specs = [kv_spec(nw - 1 - n) for n in range(nw)]
    return pl.pallas_call(
        _att_prompt_kernel,
        grid=(ATT_HEADS, nqb),
        in_specs=[pl.BlockSpec(blk, lambda h, i: (i, h))] + kv_specs + kv_specs + [
            pl.BlockSpec((None, ATT_QB, ATT_KW), lambda h, i: (h, 0, 0)),
        ],
        out_specs=pl.BlockSpec(blk, lambda h, i: (i, h)),
        out_shape=jax.ShapeDtypeStruct((n_rows, ATT_HEADS * ATT_HD), BF16),
        compiler_params=_params("parallel", "arbitrary"),
        name="att_prompt",
    )(q, k, k, k, v, v, v, bias)


def _att_sample_kernel(q_ref, kn_ref, vn_ref, kc_ref, vc_ref, bc_ref, bn_ref, o_ref):
    scale = ATT_HD ** -0.5
    dims = (((1,), (1,)), ((), ()))
    for h in range(ATT_HEADS):
        cols = slice(h * ATT_HD, (h + 1) * ATT_HD)
        q = q_ref[:, cols]
        kc = kc_ref[:, cols].astype(BF16)
        vc = vc_ref[:, cols].astype(BF16)
        sc = lax.dot_general(q, kc, dims, preferred_element_type=F32) * scale + bc_ref[h]
        sn = lax.dot_general(q, kn_ref[:, cols], dims, preferred_element_type=F32) * scale + bn_ref[h]
        m = jnp.maximum(jnp.max(sc, axis=-1, keepdims=True), jnp.max(sn, axis=-1, keepdims=True))
        pc = jnp.exp(sc - m)
        pn = jnp.exp(sn - m)
        l = jnp.sum(pc, axis=-1, keepdims=True) + jnp.sum(pn, axis=-1, keepdims=True)
        o = jnp.dot(pc.astype(BF16), vc, preferred_element_type=F32)
        o = o + jnp.dot(pn.astype(BF16), vn_ref[:, cols], preferred_element_type=F32)
        o_ref[:, cols] = (o / l).astype(o_ref.dtype)


def _att_sample(q, k, v, row0, cache_k, cache_v, bias_c, bias_n):
    nb, lc, d = cache_k.shape
    ts = (q.shape[0] - row0) // nb
    rb0 = row0 // ts
    new_spec = pl.BlockSpec((ts, d), lambda b: (b + rb0, 0))
    cache_spec = pl.BlockSpec((None, lc, d), lambda b: (b, 0, 0))
    return pl.pallas_call(
        _att_sample_kernel,
        grid=(nb,),
        in_specs=[new_spec, new_spec, new_spec, cache_spec, cache_spec,
                  pl.BlockSpec((ATT_HEADS, ts, lc), lambda b: (0, 0, 0)),
                  pl.BlockSpec((ATT_HEADS, ts, ts), lambda b: (0, 0, 0))],
        out_specs=pl.BlockSpec((ts, d), lambda b: (b, 0)),
        out_shape=jax.ShapeDtypeStruct((nb * ts, d), BF16),
        compiler_params=_params("parallel"),
        name="att_sample",
    )(q, k, v, cache_k, cache_v, bias_c, bias_n)


def _band_bias(rel_table, nq, nk, chunk_band):
    i = jnp.arange(nq, dtype=jnp.int32)[:, None]
    j = jnp.arange(nk, dtype=jnp.int32)[None, :]
    rel = jnp.clip(i - j + BAND_PAST, -REL_CLIP, REL_CLIP) + REL_CLIP
    bias = rel_table[:, rel].astype(F32)
    if chunk_band:
        qc, kc = i // CHUNK, j // CHUNK
        visible = (kc >= qc) & (kc <= qc + N_PREV_CHUNKS)
        bias = jnp.where(visible[None], bias, MASK_VALUE)
    return bias


def kernel(x_prompt, x_sample, cache_conv, state_ret, cache_att_k, cache_att_v, g_ffn1, w_ffn1_gu, w_ffn1_down, g_mix, g_ffn2, w_ffn2_gu, w_ffn2_down, w_conv_in, w_conv_dw, b_conv_dw, g_conv_norm, w_conv_out, w_ret_in, g_ret_out, w_ret_out, w_att_qkv, g_att_q, g_att_k, att_rel_bias, w_att_out):
    bp, tp, d = x_prompt.shape
    bs, ts, _ = x_sample.shape
    depth = g_mix.shape[0]
    n_p = bp * tp
    n_s = bs * ts
    assert bp == 1, "the prompt path carries conv / retention state along one sequence"

    x = jnp.concatenate([x_prompt.reshape(n_p, d), x_sample.reshape(n_s, d)], axis=0)

    pos = jnp.concatenate([jnp.arange(tp, dtype=jnp.int32),
                           jnp.tile(PAST_LEN + jnp.arange(ts, dtype=jnp.int32), bs)])
    half = RET_DK // 2
    freq = ROPE_BASE ** (-jnp.arange(half, dtype=F32) / half)
    ang = pos.astype(F32)[:, None] * freq[None, :]
    rope = (jnp.cos(ang), jnp.sin(ang))
    log_g = jnp.log1p(-jnp.exp2(-5.0 - jnp.arange(RET_HEADS, dtype=F32)))
    l_p = RET_L if tp % RET_L == 0 else tp
    l_s = ts

    conv_p, conv_s, ret_p, ret_s = [], [], [], []
    kp_l, vp_l, ks_l, vs_l = [], [], [], []
    for i in range(depth):
        kind, j = i % N_MIXERS, i // N_MIXERS
        x = _ffn(x, g_ffn1[i], w_ffn1_gu[i].astype(BF16), w_ffn1_down[i].astype(BF16))
        if kind == 0:
            u = _glu(x, g_mix[i], w_conv_in[j].astype(BF16))
            z_p = _conv_prompt(u, n_p, w_conv_dw[j], b_conv_dw[j], g_conv_norm[j])
            z_s = _conv_sample(u, n_p, cache_conv[j], w_conv_dw[j], b_conv_dw[j], g_conv_norm[j])
            x = _out_proj(x, jnp.concatenate([z_p, z_s], axis=0), w_conv_out[j].astype(BF16))
            conv_p.append(u[n_p - CONV_HALO:n_p].reshape(bp, CONV_HALO, d))
            conv_s.append(u[n_p:].reshape(bs, ts, d)[:, ts - CONV_HALO:])
        elif kind == 1:
            w_in = w_ret_in[j].astype(BF16)
            qd, vd = RET_HEADS * RET_DK, RET_HEADS * RET_DV
            (qk,) = _proj(x, g_mix[i], w_in, 0, 2 * qd, [BF16], rope=rope, k_scale_from=qd // PROJ_TN)
            (v,) = _proj(x, g_mix[i], w_in, 2 * qd, vd, [BF16])
            (sgate,) = _proj(x, g_mix[i], w_in, 2 * qd + vd, vd, [BF16], act="silu")
            zero_state = jnp.zeros((bp, RET_HEADS, RET_DK, RET_DV), F32)
            o_p, s_p = _retention(qk, v, sgate, zero_state, g_ret_out[j], 0, bp, tp, l_p,
                                  (log_g, jnp.exp(log_g * l_p)))
            o_s, s_s = _retention(qk, v, sgate, state_ret[j], g_ret_out[j], n_p, bs, ts, l_s,
                                  (log_g, jnp.exp(log_g * l_s)))
            x = _out_proj(x, jnp.concatenate([o_p, o_s], axis=0), w_ret_out[j].astype(BF16))
            ret_p.append(s_p)
            ret_s.append(s_s)
        else:
            w_qkv = w_att_qkv[j].astype(BF16)
            (q,) = _proj(x, g_mix[i], w_qkv, 0, d, [BF16], headnorm=g_att_q[j])
            k, k32 = _proj(x, g_mix[i], w_qkv, d, d, [BF16, F32], headnorm=g_att_k[j])
            v, v32 = _proj(x, g_mix[i], w_qkv, 2 * d, d, [BF16, F32])
            lc = cache_att_k.shape[2]
            o_p = _att_prompt(q, k, v, _band_bias(att_rel_bias[j], ATT_QB, ATT_KW, True), n_p)
            bias_s = _band_bias(att_rel_bias[j], ts, lc + ts, False)
            o_s = _att_sample(q, k, v, n_p, cache_att_k[j].reshape(bs, lc, d), cache_att_v[j].reshape(bs, lc, d),
                              bias_s[:, :, :lc], bias_s[:, :, lc:])
            x = _out_proj(x, jnp.concatenate([o_p, o_s], axis=0), w_att_out[j].astype(BF16))
            keep = min(BAND_PAST, tp)
            kp_l.append(k32[n_p - keep:n_p].reshape(bp, keep, ATT_HEADS, ATT_HD))
            vp_l.append(v32[n_p - keep:n_p].reshape(bp, keep, ATT_HEADS, ATT_HD))
            ks_l.append(k32[n_p:].reshape(bs, ts, ATT_HEADS, ATT_HD))
            vs_l.append(v32[n_p:].reshape(bs, ts, ATT_HEADS, ATT_HD))
        x = _ffn(x, g_ffn2[i], w_ffn2_gu[i].astype(BF16), w_ffn2_down[i].astype(BF16))

    return (x[:n_p].reshape(bp, tp, d), x[n_p:].reshape(bs, ts, d),
            jnp.stack(conv_p), jnp.stack(conv_s),
            jnp.stack(ret_p), jnp.stack(ret_s),
            jnp.stack(kp_l), jnp.stack(vp_l),
            jnp.stack(ks_l), jnp.stack(vs_l))
```

```python
import functools

import jax
import jax.numpy as jnp
from jax import lax
from jax.experimental import pallas as pl
from jax.experimental.pallas import tpu as pltpu

F32 = jnp.float32
BF16 = jnp.bfloat16

CHUNK = 64
NORM_EPS = 1e-6
CONV_WIDTH = 31
CONV_HALO = CONV_WIDTH - 1
RET_HEADS = 8
RET_DK = 256
RET_DV = 512
ROPE_BASE = 10000.0
ATT_HEADS = 16
ATT_HD = 128
N_PREV_CHUNKS = 8
BAND_PAST = N_PREV_CHUNKS * CHUNK
REL_CLIP = 256
PAST_LEN = 4096
N_MIXERS = 3

LANES = 128
SUBLANES = 8
VMEM_LIMIT_BYTES = 56 * 1024 * 1024
MASK_VALUE = -1e30

FFN_TM = 512
FFN_TF = 512
PROJ_TM = 1024
PROJ_TN = 512
CONV_TT = 256
CONV_ROWS = 64
CONV_COLS = 128
CONV_PAD = 32
RET_L = 256
ATT_QB = 256
ATT_KW = ATT_QB + BAND_PAST


def _params(*sem):
    return pltpu.CompilerParams(dimension_semantics=sem, vmem_limit_bytes=VMEM_LIMIT_BYTES)


def _rmsnorm_rows(x, g):
    ms = jnp.mean(x * x, axis=-1, keepdims=True)
    return x * lax.rsqrt(ms + NORM_EPS) * g


def _silu(x):
    return x * jax.nn.sigmoid(x)


def _row_tile_specs(tm, width, n_pt):
    whole = pl.BlockSpec((tm, width), lambda i, j: (i, 0))
    prompt = pl.BlockSpec((tm, width), lambda i, j: (jnp.minimum(i, n_pt - 1), 0))
    sample = pl.BlockSpec((tm, width), lambda i, j: (jnp.maximum(i - n_pt, 0), 0))
    return whole, prompt, sample


def _ffn_kernel(*refs, n_pt, split_in, split_out):
    refs = list(refs)
    x_refs = [refs.pop(0) for _ in range(2 if split_in else 1)]
    g_ref, wg_ref, wu_ref, wd_ref = refs[:4]
    o_refs = refs[4:-1]
    h_ref = refs[-1]
    i = pl.program_id(0)
    j = pl.program_id(1)

    def body(x_ref, o_ref):
        @pl.when(j == 0)
        def _():
            x = x_ref[...]
            h_ref[...] = _rmsnorm_rows(x, g_ref[...]).astype(BF16)
            o_ref[...] = x

        h = h_ref[...]
        gate = jnp.dot(h, wg_ref[...], preferred_element_type=F32)
        up = jnp.dot(h, wu_ref[...], preferred_element_type=F32)
        a = (_silu(gate) * (0.5 * up)).astype(BF16)
        o_ref[...] += jnp.dot(a, wd_ref[...], preferred_element_type=F32)

    if not (split_in or split_out):
        body(x_refs[0], o_refs[0])
    else:
        pl.when(i < n_pt)(lambda: body(x_refs[0], o_refs[0]))
        pl.when(i >= n_pt)(lambda: body(x_refs[-1], o_refs[-1]))


def _ffn(xs, g, w_gu, w_down, layer, n_p, split_out=False):
    split_in = len(xs) == 2
    d = xs[0].shape[1]
    t = sum(x.shape[0] for x in xs)
    nf = w_down.shape[1] // FFN_TF
    tm = FFN_TM
    whole, prompt, sample = _row_tile_specs(tm, d, n_p // tm)
    out_shape = jax.ShapeDtypeStruct((t, d), F32)
    if split_out:
        out_shape = [jax.ShapeDtypeStruct((n_p, d), F32), jax.ShapeDtypeStruct((t - n_p, d), F32)]
    kern = functools.partial(_ffn_kernel, n_pt=n_p // tm, split_in=split_in, split_out=split_out)
    return pl.pallas_call(
        kern,
        grid=(t // tm, nf),
        in_specs=([prompt, sample] if split_in else [whole]) + [
            pl.BlockSpec((1, d), lambda i, j: (0, 0)),
            pl.BlockSpec((None, d, FFN_TF), lambda i, j: (layer, 0, j)),
            pl.BlockSpec((None, d, FFN_TF), lambda i, j: (layer, 0, j + nf)),
            pl.BlockSpec((None, FFN_TF, d), lambda i, j: (layer, j, 0)),
        ],
        out_specs=[prompt, sample] if split_out else whole,
        out_shape=out_shape,
        scratch_shapes=[pltpu.VMEM((tm, d), BF16)],
        compiler_params=_params("arbitrary", "arbitrary"),
        name="ffn",
    )(*xs, g.reshape(1, d), w_gu, w_gu, w_down)


def _proj_kernel(*refs, act, rope, k_scale_from, headnorm, n_out):
    it = iter(refs)
    x_ref, g_ref, w_ref = next(it), next(it), next(it)
    cos_ref = sin_ref = gh_ref = None
    if rope:
        cos_ref, sin_ref = next(it), next(it)
    if headnorm:
        gh_ref = next(it)
    out_refs = [next(it) for _ in range(n_out)]
    h_ref = next(it)
    j = pl.program_id(1)

    @pl.when(j == 0)
    def _():
        h_ref[...] = _rmsnorm_rows(x_ref[...], g_ref[...]).astype(BF16)

    y = jnp.dot(h_ref[...], w_ref[...], preferred_element_type=F32)
    tn = y.shape[1]
    if act == "silu":
        y = _silu(y)
    if rope:
        cos, sin = cos_ref[...], sin_ref[...]
        scale = jnp.where(j >= k_scale_from, RET_DK ** -0.5, 1.0).astype(F32)
        half = RET_DK // 2
        parts = []
        for hh in range(tn // RET_DK):
            x1 = y[:, hh * RET_DK: hh * RET_DK + half]
            x2 = y[:, hh * RET_DK + half: (hh + 1) * RET_DK]
            parts.append((x1 * cos - x2 * sin) * scale)
            parts.append((x1 * sin + x2 * cos) * scale)
        y = jnp.concatenate(parts, axis=1)
    if headnorm:
        gh = gh_ref[...]
        parts = []
        for hh in range(tn // ATT_HD):
            parts.append(_rmsnorm_rows(y[:, hh * ATT_HD:(hh + 1) * ATT_HD], gh))
        y = jnp.concatenate(parts, axis=1)
    for o_ref in out_refs:
        o_ref[...] = y.astype(o_ref.dtype)


def _proj(x, g, w, layer, col0, ncols, out_dtypes, *, act=None, rope=None, k_scale_from=0, headnorm=None):
    t, d = x.shape
    tm, tn = PROJ_TM, PROJ_TN
    cb0 = col0 // tn
    in_specs = [
        pl.BlockSpec((tm, d), lambda i, j: (i, 0)),
        pl.BlockSpec((1, d), lambda i, j: (0, 0)),
        pl.BlockSpec((None, d, tn), lambda i, j: (layer, 0, j + cb0)),
    ]
    args = [x, g.reshape(1, d), w]
    if rope is not None:
        half = RET_DK // 2
        in_specs += [pl.BlockSpec((tm, half), lambda i, j: (i, 0))] * 2
        args += list(rope)
    if headnorm is not None:
        in_specs.append(pl.BlockSpec((1, ATT_HD), lambda i, j: (0, 0)))
        args.append(headnorm.reshape(1, ATT_HD))
    kern = functools.partial(_proj_kernel, act=act, rope=rope is not None, k_scale_from=k_scale_from,
                             headnorm=headnorm is not None, n_out=len(out_dtypes))
    outs = pl.pallas_call(
        kern,
        grid=(t // tm, ncols // tn),
        in_specs=in_specs,
        out_specs=[pl.BlockSpec((tm, tn), lambda i, j: (i, j)) for _ in out_dtypes],
        out_shape=[jax.ShapeDtypeStruct((t, ncols), dt) for dt in out_dtypes],
        scratch_shapes=[pltpu.VMEM((tm, d), BF16)],
        compiler_params=_params("parallel", "arbitrary"),
        name="proj",
    )(*args)
    return outs


def _glu_kernel(x_ref, g_ref, wa_ref, wb_ref, o_ref, h_ref):
    @pl.when(pl.program_id(1) == 0)
    def _():
        h_ref[...] = _rmsnorm_rows(x_ref[...], g_ref[...]).astype(BF16)

    h = h_ref[...]
    a = jnp.dot(h, wa_ref[...], preferred_element_type=F32)
    b = jnp.dot(h, wb_ref[...], preferred_element_type=F32)
    o_ref[...] = a * jax.nn.sigmoid(b)


def _glu(x, g, w_in, layer):
    t, d = x.shape
    tm, tn = PROJ_TM, PROJ_TN
    nb = d // tn
    return pl.pallas_call(
        _glu_kernel,
        grid=(t // tm, nb),
        in_specs=[
            pl.BlockSpec((tm, d), lambda i, j: (i, 0)),
            pl.BlockSpec((1, d), lambda i, j: (0, 0)),
            pl.BlockSpec((None, d, tn), lambda i, j: (layer, 0, j)),
            pl.BlockSpec((None, d, tn), lambda i, j: (layer, 0, j + nb)),
        ],
        out_specs=pl.BlockSpec((tm, tn), lambda i, j: (i, j)),
        out_shape=jax.ShapeDtypeStruct((t, d), F32),
        scratch_shapes=[pltpu.VMEM((tm, d), BF16)],
        compiler_params=_params("parallel", "arbitrary"),
        name="conv_glu",
    )(x, g.reshape(1, d), w_in, w_in)


def _out_proj_kernel(x_ref, yp_ref, ys_ref, w_ref, o_ref, *, n_pt):
    i = pl.program_id(0)

    def body(y_ref):
        o_ref[...] = x_ref[...] + jnp.dot(y_ref[...], w_ref[...], preferred_element_type=F32)

    pl.when(i < n_pt)(lambda: body(yp_ref))
    pl.when(i >= n_pt)(lambda: body(ys_ref))


def _out_proj(x, y_p, y_s, w, layer):
    t, d = x.shape
    k = y_p.shape[1]
    tm, tn = PROJ_TM, PROJ_TN
    n_pt = y_p.shape[0] // tm
    _, prompt, sample = _row_tile_specs(tm, k, n_pt)
    return pl.pallas_call(
        functools.partial(_out_proj_kernel, n_pt=n_pt),
        grid=(t // tm, d // tn),
        in_specs=[
            pl.BlockSpec((tm, tn), lambda i, j: (i, j)),
            prompt,
            sample,
            pl.BlockSpec((None, k, tn), lambda i, j: (layer, 0, j)),
        ],
        out_specs=pl.BlockSpec((tm, tn), lambda i, j: (i, j)),
        out_shape=jax.ShapeDtypeStruct((t, d), F32),
        compiler_params=_params("parallel", "arbitrary"),
        name="out_proj",
    )(x, y_p, y_s, w)


def _conv_kernel(*refs, tt, carry):
    if carry:
        u_ref, w_ref, b_ref, gn_ref, z_ref, ext_ref, y_ref = refs
        halo_ref = None
    else:
        u_ref, halo_ref, w_ref, b_ref, gn_ref, z_ref, ext_ref, y_ref = refs
    d = u_ref.shape[-1]
    off = CONV_PAD - CONV_HALO
    rows = min(CONV_ROWS, tt)
    slab = rows + CONV_PAD

    if carry:
        @pl.when(pl.program_id(0) == 0)
        def _():
            ext_ref[0:CONV_PAD, :] = jnp.zeros((CONV_PAD, d), F32)
    else:
        ext_ref[0:SUBLANES, :] = jnp.zeros((SUBLANES, d), F32)
        ext_ref[off:CONV_PAD, :] = halo_ref[...]
    ext_ref[CONV_PAD:CONV_PAD + tt, :] = u_ref[...]

    def row_chunk(r, _):
        r0 = pl.multiple_of(r * rows, rows)
        for c0 in range(0, d, CONV_COLS):
            cols = slice(c0, c0 + CONV_COLS)
            x = ext_ref[pl.ds(r0, slab), cols]
            acc = jnp.zeros((rows, CONV_COLS), F32)
            for s in range(SUBLANES):
                xs = x if s == 0 else pltpu.roll(x, slab - s, 0)
                for a8 in range(0, CONV_PAD + SUBLANES, SUBLANES):
                    w = a8 + s - off
                    if 0 <= w < CONV_WIDTH:
                        acc = acc + xs[a8:a8 + rows, :] * w_ref[w:w + 1, cols]
            y_ref[pl.ds(r0, rows), cols] = acc + b_ref[:, cols]
        return 0

    lax.fori_loop(0, tt // rows, row_chunk, 0)

    z_ref[...] = _silu(_rmsnorm_rows(y_ref[...], gn_ref[...])).astype(z_ref.dtype)

    if carry:
        ext_ref[0:CONV_PAD, :] = ext_ref[tt:tt + CONV_PAD, :]


def _conv_prompt(u, n_rows, w_dw, b_dw, g_norm, layer):
    d = u.shape[1]
    tt = CONV_TT
    kern = functools.partial(_conv_kernel, tt=tt, carry=True)
    return pl.pallas_call(
        kern,
        grid=(n_rows // tt,),
        in_specs=[
            pl.BlockSpec((tt, d), lambda i: (i, 0)),
            pl.BlockSpec((None, CONV_WIDTH, d), lambda i: (layer, 0, 0)),
            pl.BlockSpec((1, d), lambda i: (0, 0)),
            pl.BlockSpec((1, d), lambda i: (0, 0)),
        ],
        out_specs=pl.BlockSpec((tt, d), lambda i: (i, 0)),
        out_shape=jax.ShapeDtypeStruct((n_rows, d), BF16),
        scratch_shapes=[pltpu.VMEM((CONV_PAD + tt, d), F32), pltpu.VMEM((tt, d), F32)],
        compiler_params=_params("arbitrary"),
        name="conv_prompt",
    )(u, w_dw, b_dw.reshape(1, d), g_norm.reshape(1, d))


def _conv_sample(u, row0, cache, w_dw, b_dw, g_norm, layer):
    _, nb, _, d = cache.shape
    tt = (u.shape[0] - row0) // nb
    rb0 = row0 // tt
    kern = functools.partial(_conv_kernel, tt=tt, carry=False)
    return pl.pallas_call(
        kern,
        grid=(nb,),
        in_specs=[
            pl.BlockSpec((tt, d), lambda b: (b + rb0, 0)),
            pl.BlockSpec((None, None, CONV_HALO, d), lambda b: (layer, b, 0, 0)),
            pl.BlockSpec((None, CONV_WIDTH, d), lambda b: (layer, 0, 0)),
            pl.BlockSpec((1, d), lambda b: (0, 0)),
            pl.BlockSpec((1, d), lambda b: (0, 0)),
        ],
        out_specs=pl.BlockSpec((tt, d), lambda b: (b, 0)),
        out_shape=jax.ShapeDtypeStruct((nb * tt, d), BF16),
        scratch_shapes=[pltpu.VMEM((CONV_PAD + tt, d), F32), pltpu.VMEM((tt, d), F32)],
        compiler_params=_params("parallel"),
        name="conv_sample",
    )(u, cache, w_dw, b_dw.reshape(1, d), g_norm.reshape(1, d))


def _ret_kernel(lg_ref, gl_ref, q_ref, k_ref, v_ref, sg_ref, s0_ref, go_ref, o_ref, sout_ref, s_ref, *, L):
    h = pl.program_id(1)
    c = pl.program_id(2)
    lg = lg_ref[h]
    g_chunk = gl_ref[h]

    @pl.when(c == 0)
    def _():
        s_ref[...] = s0_ref[...]

    q, k, v = q_ref[...], k_ref[...], v_ref[...]
    ri = lax.broadcasted_iota(jnp.int32, (L, L), 0)
    ci = lax.broadcasted_iota(jnp.int32, (L, L), 1)
    diff = (ri - ci).astype(F32)
    decay = jnp.where(diff >= 0, jnp.exp(lg * jnp.maximum(diff, 0.0)), 0.0)
    inner = lax.dot_general(q, k, (((1,), (1,)), ((), ())), preferred_element_type=F32) * decay
    o = jnp.dot(inner.astype(BF16), v, preferred_element_type=F32)

    state = s_ref[...]
    row = lax.broadcasted_iota(jnp.int32, (L, 1), 0).astype(F32)
    q_dec = jnp.exp(lg * (row + 1.0))
    o = o + jnp.dot(q, state.astype(BF16), preferred_element_type=F32) * q_dec
    k_dec = jnp.exp(lg * (L - 1.0 - row))
    kd = (k.astype(F32) * k_dec).astype(BF16)
    new_state = state * g_chunk + lax.dot_general(kd, v, (((0,), (0,)), ((), ())), preferred_element_type=F32)
    s_ref[...] = new_state

    @pl.when(c == pl.num_programs(2) - 1)
    def _():
        sout_ref[...] = new_state

    o = _rmsnorm_rows(o, go_ref[...])
    o_ref[...] = (sg_ref[...].astype(F32) * o).astype(o_ref.dtype)


def _retention(qk, v, sgate, s0, layer, g_out, row0, n_seq, seq_len, L, decay_consts):
    nc = seq_len // L
    rb0 = row0 // L
    h_, dk, dv = RET_HEADS, RET_DK, RET_DV
    row_map = lambda b, h, c: b * nc + c + rb0
    kern = functools.partial(_ret_kernel, L=L)
    lg, gl = decay_consts
    o, s_out = pl.pallas_call(
        kern,
        grid=(n_seq, h_, nc),
        in_specs=[
            pl.BlockSpec(memory_space=pltpu.SMEM),
            pl.BlockSpec(memory_space=pltpu.SMEM),
            pl.BlockSpec((L, dk), lambda b, h, c: (row_map(b, h, c), h)),
            pl.BlockSpec((L, dk), lambda b, h, c: (row_map(b, h, c), h + h_)),
            pl.BlockSpec((L, dv), lambda b, h, c: (row_map(b, h, c), h)),
            pl.BlockSpec((L, dv), lambda b, h, c: (row_map(b, h, c), h)),
            pl.BlockSpec((None, None, None, dk, dv), lambda b, h, c: (layer, b, h, 0, 0)),
            pl.BlockSpec((1, dv), lambda b, h, c: (0, h)),
        ],
        out_specs=[
            pl.BlockSpec((L, dv), lambda b, h, c: (b * nc + c, h)),
            pl.BlockSpec((None, None, dk, dv), lambda b, h, c: (b, h, 0, 0)),
        ],
        out_shape=[
            jax.ShapeDtypeStruct((n_seq * seq_len, h_ * dv), BF16),
            jax.ShapeDtypeStruct((n_seq, h_, dk, dv), F32),
        ],
        scratch_shapes=[pltpu.VMEM((dk, dv), F32)],
        compiler_params=_params("parallel", "parallel", "arbitrary"),
        name="retention",
    )(lg, gl, qk, qk, v, sgate, s0, g_out.reshape(1, h_ * dv))
    return o, s_out


def _softmax_weights(s):
    m = jnp.max(s, axis=-1, keepdims=True)
    p = jnp.exp(s - m)
    return p, jnp.sum(p, axis=-1, keepdims=True)


def _att_prompt_kernel(q_ref, k0_ref, k1_ref, k2_ref, v0_ref, v1_ref, v2_ref, bias_ref, o_ref):
    qb = pl.program_id(1)
    q = q_ref[...]
    k = jnp.concatenate([k0_ref[...], k1_ref[...], k2_ref[...]], axis=0)
    v = jnp.concatenate([v0_ref[...], v1_ref[...], v2_ref[...]], axis=0)
    s = lax.dot_general(q, k, (((1,), (1,)), ((), ())), preferred_element_type=F32)
    s = s * (ATT_HD ** -0.5) + bias_ref[...]
    k_pos = qb * ATT_QB - BAND_PAST + lax.broadcasted_iota(jnp.int32, s.shape, 1)
    s = jnp.where(k_pos >= 0, s, MASK_VALUE)
    p, l = _softmax_weights(s)
    o = jnp.dot(p.astype(BF16), v, preferred_element_type=F32)
    o_ref[...] = (o / l).astype(o_ref.dtype)


def _att_prompt(q, k, v, bias, n_rows):
    nqb = n_rows // ATT_QB
    nw = ATT_KW // ATT_QB
    blk = (ATT_QB, ATT_HD)

    def kv_spec(back):
        return pl.BlockSpec(blk, lambda h, i: (jnp.maximum(i - back, 0), h))

    kv_specs = [kv_spec(nw - 1 - n) for n in range(nw)]
    return pl.pallas_call(
        _att_prompt_kernel,
        grid=(ATT_HEADS, nqb),
        in_specs=[pl.BlockSpec(blk, lambda h, i: (i, h))] + kv_specs + kv_specs + [
            pl.BlockSpec((None, ATT_QB, ATT_KW), lambda h, i: (h, 0, 0)),
        ],
        out_specs=pl.BlockSpec(blk, lambda h, i: (i, h)),
        out_shape=jax.ShapeDtypeStruct((n_rows, ATT_HEADS * ATT_HD), BF16),
        compiler_params=_params("parallel", "arbitrary"),
        name="att_prompt",
    )(q, k, k, k, v, v, v, bias)


def _att_sample_kernel(q_ref, kn_ref, vn_ref, kc_ref, vc_ref, bc_ref, bn_ref, o_ref):
    scale = ATT_HD ** -0.5
    dims = (((1,), (1,)), ((), ()))
    for h in range(ATT_HEADS):
        cols = slice(h * ATT_HD, (h + 1) * ATT_HD)
        q = q_ref[:, cols]
        lc = kc_ref.shape[0] // ATT_HEADS
        kc = kc_ref[pl.ds(h, lc, stride=ATT_HEADS), :].astype(BF16)
        vc = vc_ref[pl.ds(h, lc, stride=ATT_HEADS), :].astype(BF16)
        sc = lax.dot_general(q, kc, dims, preferred_element_type=F32) * scale + bc_ref[h]
        sn = lax.dot_general(q, kn_ref[:, cols], dims, preferred_element_type=F32) * scale + bn_ref[h]
        m = jnp.maximum(jnp.max(sc, axis=-1, keepdims=True), jnp.max(sn, axis=-1, keepdims=True))
        pc = jnp.exp(sc - m)
        pn = jnp.exp(sn - m)
        l = jnp.sum(pc, axis=-1, keepdims=True) + jnp.sum(pn, axis=-1, keepdims=True)
        o = jnp.dot(pc.astype(BF16), vc, preferred_element_type=F32)
        o = o + jnp.dot(pn.astype(BF16), vn_ref[:, cols], preferred_element_type=F32)
        o_ref[:, cols] = (o / l).astype(o_ref.dtype)


def _att_sample(q, k, v, row0, cache_k, cache_v, layer, bias_c, bias_n):
    _, nb, lc, nh, hd = cache_k.shape
    d = nh * hd
    ts = (q.shape[0] - row0) // nb
    rb0 = row0 // ts
    new_spec = pl.BlockSpec((ts, d), lambda b: (b + rb0, 0))
    cache_k = cache_k.reshape(cache_k.shape[0], nb, lc * nh, hd)
    cache_v = cache_v.reshape(cache_v.shape[0], nb, lc * nh, hd)
    cache_spec = pl.BlockSpec((None, None, lc * nh, hd), lambda b: (layer, b, 0, 0))
    return pl.pallas_call(
        _att_sample_kernel,
        grid=(nb,),
        in_specs=[new_spec, new_spec, new_spec, cache_spec, cache_spec,
                  pl.BlockSpec((nh, ts, lc), lambda b: (0, 0, 0)),
                  pl.BlockSpec((nh, ts, ts), lambda b: (0, 0, 0))],
        out_specs=pl.BlockSpec((ts, d), lambda b: (b, 0)),
        out_shape=jax.ShapeDtypeStruct((nb * ts, d), BF16),
        compiler_params=_params("parallel"),
        name="att_sample",
    )(q, k, v, cache_k, cache_v, bias_c, bias_n)


def _band_bias(rel_table, nq, nk, chunk_band):
    nh = rel_table.shape[0]
    n = nq + nk - 1
    q_minus_k = nq - 1 - jnp.arange(n, dtype=jnp.int32) + BAND_PAST
    r = rel_table[:, jnp.clip(q_minus_k, -REL_CLIP, REL_CLIP) + REL_CLIP].astype(F32)
    r = jnp.pad(r, ((0, 0), (0, 1)))
    skew = jnp.tile(r, (1, nq))[:, :nq * n].reshape(nh, nq, n)
    bias = skew[:, :, nq - 1:nq - 1 + nk]
    if chunk_band:
        qc = jnp.arange(nq, dtype=jnp.int32)[:, None] // CHUNK
        kc = jnp.arange(nk, dtype=jnp.int32)[None, :] // CHUNK
        visible = (kc >= qc) & (kc <= qc + N_PREV_CHUNKS)
        bias = jnp.where(visible[None], bias, MASK_VALUE)
    return bias


def kernel(x_prompt, x_sample, cache_conv, state_ret, cache_att_k, cache_att_v, g_ffn1, w_ffn1_gu, w_ffn1_down, g_mix, g_ffn2, w_ffn2_gu, w_ffn2_down, w_conv_in, w_conv_dw, b_conv_dw, g_conv_norm, w_conv_out, w_ret_in, g_ret_out, w_ret_out, w_att_qkv, g_att_q, g_att_k, att_rel_bias, w_att_out):
    bp, tp, d = x_prompt.shape
    bs, ts, _ = x_sample.shape
    depth = g_mix.shape[0]
    n_p = bp * tp
    n_s = bs * ts
    assert bp == 1, "the prompt path carries conv / retention state along one sequence"

    w_ffn1_gu, w_ffn1_down, w_ffn2_gu, w_ffn2_down, w_conv_in, w_conv_out, w_ret_in, w_ret_out, w_att_qkv, w_att_out = (
        w.astype(BF16) for w in (w_ffn1_gu, w_ffn1_down, w_ffn2_gu, w_ffn2_down, w_conv_in, w_conv_out,
                                 w_ret_in, w_ret_out, w_att_qkv, w_att_out))

    pos = jnp.concatenate([jnp.arange(tp, dtype=jnp.int32),
                           jnp.tile(PAST_LEN + jnp.arange(ts, dtype=jnp.int32), bs)])
    half = RET_DK // 2
    freq = ROPE_BASE ** (-jnp.arange(half, dtype=F32) / half)
    ang = pos.astype(F32)[:, None] * freq[None, :]
    rope = (jnp.cos(ang), jnp.sin(ang))
    log_g = jnp.log1p(-jnp.exp2(-5.0 - jnp.arange(RET_HEADS, dtype=F32)))
    l_p = RET_L if tp % RET_L == 0 else tp
    l_s = ts

    conv_p, conv_s, ret_p, ret_s = [], [], [], []
    kp_l, vp_l, ks_l, vs_l = [], [], [], []
    xs = [x_prompt.reshape(n_p, d), x_sample.reshape(n_s, d)]
    for i in range(depth):
        kind, j = i % N_MIXERS, i // N_MIXERS
        x = _ffn(xs, g_ffn1[i], w_ffn1_gu, w_ffn1_down, i, n_p)
        if kind == 0:
            u = _glu(x, g_mix[i], w_conv_in, j)
            z_p = _conv_prompt(u, n_p, w_conv_dw, b_conv_dw[j], g_conv_norm[j], j)
            z_s = _conv_sample(u, n_p, cache_conv, w_conv_dw, b_conv_dw[j], g_conv_norm[j], j)
            x = _out_proj(x, z_p, z_s, w_conv_out, j)
            conv_p.append(u[n_p - CONV_HALO:n_p].reshape(bp, CONV_HALO, d))
            conv_s.append(u[n_p:].reshape(bs, ts, d)[:, ts - CONV_HALO:])
        elif kind == 1:
            qd, vd = RET_HEADS * RET_DK, RET_HEADS * RET_DV
            (qk,) = _proj(x, g_mix[i], w_ret_in, j, 0, 2 * qd, [BF16], rope=rope, k_scale_from=qd // PROJ_TN)
            (v,) = _proj(x, g_mix[i], w_ret_in, j, 2 * qd, vd, [BF16])
            (sgate,) = _proj(x, g_mix[i], w_ret_in, j, 2 * qd + vd, vd, [BF16], act="silu")
            zero_state = jnp.zeros((1, bp, RET_HEADS, RET_DK, RET_DV), F32)
            o_p, s_p = _retention(qk, v, sgate, zero_state, 0, g_ret_out[j], 0, bp, tp, l_p,
                                  (log_g, jnp.exp(log_g * l_p)))
            o_s, s_s = _retention(qk, v, sgate, state_ret, j, g_ret_out[j], n_p, bs, ts, l_s,
                                  (log_g, jnp.exp(log_g * l_s)))
            x = _out_proj(x, o_p, o_s, w_ret_out, j)
            ret_p.append(s_p)
            ret_s.append(s_s)
        else:
            (q,) = _proj(x, g_mix[i], w_att_qkv, j, 0, d, [BF16], headnorm=g_att_q[j])
            k, k32 = _proj(x, g_mix[i], w_att_qkv, j, d, d, [BF16, F32], headnorm=g_att_k[j])
            v, v32 = _proj(x, g_mix[i], w_att_qkv, j, 2 * d, d, [BF16, F32])
            lc = cache_att_k.shape[2]
            o_p = _att_prompt(q, k, v, _band_bias(att_rel_bias[j], ATT_QB, ATT_KW, True), n_p)
            bias_s = _band_bias(att_rel_bias[j], ts, lc + ts, False)
            o_s = _att_sample(q, k, v, n_p, cache_att_k, cache_att_v, j, bias_s[:, :, :lc], bias_s[:, :, lc:])
            x = _out_proj(x, o_p, o_s, w_att_out, j)
            keep = min(BAND_PAST, tp)
            kp_l.append(k32[n_p - keep:n_p].reshape(bp, keep, ATT_HEADS, ATT_HD))
            vp_l.append(v32[n_p - keep:n_p].reshape(bp, keep, ATT_HEADS, ATT_HD))
            ks_l.append(k32[n_p:].reshape(bs, ts, ATT_HEADS, ATT_HD))
            vs_l.append(v32[n_p:].reshape(bs, ts, ATT_HEADS, ATT_HD))
        xs = _ffn([x], g_ffn2[i], w_ffn2_gu, w_ffn2_down, i, n_p, split_out=(i == depth - 1))
        if i < depth - 1:
            xs = [xs]

    return (xs[0].reshape(bp, tp, d), xs[1].reshape(bs, ts, d),
            jnp.stack(conv_p), jnp.stack(conv_s),
            jnp.stack(ret_p), jnp.stack(ret_s),
            jnp.stack(kp_l), jnp.stack(vp_l),
            jnp.stack(ks_l), jnp.stack(vs_l))
```

```python
import functools

import jax
import jax.numpy as jnp
from jax import lax
from jax.experimental import pallas as pl
from jax.experimental.pallas import tpu as pltpu

F32 = jnp.float32
BF16 = jnp.bfloat16

CHUNK = 64
NORM_EPS = 1e-6
CONV_WIDTH = 31
CONV_HALO = CONV_WIDTH - 1
RET_HEADS = 8
RET_DK = 256
RET_DV = 512
ROPE_BASE = 10000.0
ATT_HEADS = 16
ATT_HD = 128
N_PREV_CHUNKS = 8
BAND_PAST = N_PREV_CHUNKS * CHUNK
REL_CLIP = 256
PAST_LEN = 4096
N_MIXERS = 3

LANES = 128
SUBLANES = 8
VMEM_LIMIT_BYTES = 56 * 1024 * 1024
MASK_VALUE = -1e30

FFN_TM = 1024
FFN_TM_SPLIT = 512
FFN_TF = 512
PROJ_TM = 1024
PROJ_TN = 512
CONV_TT = 256
CONV_ROWS = 64
CONV_COLS = 128
CONV_PAD = 32
RET_L = 256
RET_HP = 2
ATT_QB = 256
ATT_KW = ATT_QB + BAND_PAST
ATT_HP = 4


def _params(*sem):
    return pltpu.CompilerParams(dimension_semantics=sem, vmem_limit_bytes=VMEM_LIMIT_BYTES)


def _rmsnorm_rows(x, g):
    ms = jnp.mean(x * x, axis=-1, keepdims=True)
    return x * lax.rsqrt(ms + NORM_EPS) * g


def _silu(x):
    return x * jax.nn.sigmoid(x)


def _row_tile_specs(tm, width, n_pt):
    whole = pl.BlockSpec((tm, width), lambda i, j: (i, 0))
    prompt = pl.BlockSpec((tm, width), lambda i, j: (jnp.minimum(i, n_pt - 1), 0))
    sample = pl.BlockSpec((tm, width), lambda i, j: (jnp.maximum(i - n_pt, 0), 0))
    return whole, prompt, sample


def _ffn_kernel(*refs, n_pt, split_in, split_out):
    refs = list(refs)
    x_refs = [refs.pop(0) for _ in range(2 if split_in else 1)]
    g_ref, wg_ref, wu_ref, wd_ref = refs[:4]
    o_refs = refs[4:-1]
    h_ref = refs[-1]
    i = pl.program_id(0)
    j = pl.program_id(1)

    def body(x_ref, o_ref):
        @pl.when(j == 0)
        def _():
            x = x_ref[...]
            h_ref[...] = _rmsnorm_rows(x, g_ref[...]).astype(BF16)
            o_ref[...] = x

        h = h_ref[...]
        gate = jnp.dot(h, wg_ref[...], preferred_element_type=F32)
        up = jnp.dot(h, wu_ref[...], preferred_element_type=F32)
        a = (_silu(gate) * (0.5 * up)).astype(BF16)
        o_ref[...] += jnp.dot(a, wd_ref[...], preferred_element_type=F32)

    if not (split_in or split_out):
        body(x_refs[0], o_refs[0])
    else:
        pl.when(i < n_pt)(lambda: body(x_refs[0], o_refs[0]))
        pl.when(i >= n_pt)(lambda: body(x_refs[-1], o_refs[-1]))


def _ffn(xs, g, w_gu, w_down, layer, n_p, split_out=False):
    split_in = len(xs) == 2
    d = xs[0].shape[1]
    t = sum(x.shape[0] for x in xs)
    nf = w_down.shape[1] // FFN_TF
    tm = FFN_TM_SPLIT if (split_in or split_out) else FFN_TM
    whole, prompt, sample = _row_tile_specs(tm, d, n_p // tm)
    out_shape = jax.ShapeDtypeStruct((t, d), F32)
    if split_out:
        out_shape = [jax.ShapeDtypeStruct((n_p, d), F32), jax.ShapeDtypeStruct((t - n_p, d), F32)]
    kern = functools.partial(_ffn_kernel, n_pt=n_p // tm, split_in=split_in, split_out=split_out)
    return pl.pallas_call(
        kern,
        grid=(t // tm, nf),
        in_specs=([prompt, sample] if split_in else [whole]) + [
            pl.BlockSpec((1, d), lambda i, j: (0, 0)),
            pl.BlockSpec((None, d, FFN_TF), lambda i, j: (layer, 0, j)),
            pl.BlockSpec((None, d, FFN_TF), lambda i, j: (layer, 0, j + nf)),
            pl.BlockSpec((None, FFN_TF, d), lambda i, j: (layer, j, 0)),
        ],
        out_specs=[prompt, sample] if split_out else whole,
        out_shape=out_shape,
        scratch_shapes=[pltpu.VMEM((tm, d), BF16)],
        compiler_params=_params("arbitrary", "arbitrary"),
        name="ffn",
    )(*xs, g.reshape(1, d), w_gu, w_gu, w_down)


def _proj_kernel(*refs, segments, has_rope, has_gain, has_f32):
    it = iter(refs)
    x_ref, g_ref, w_ref = next(it), next(it), next(it)
    cos_ref = sin_ref = gh_ref = o32_ref = None
    if has_rope:
        cos_ref, sin_ref = next(it), next(it)
    if has_gain:
        gh_ref = next(it)
    o_ref = next(it)
    if has_f32:
        o32_ref = next(it)
    h_ref = next(it)
    j = pl.program_id(1)

    @pl.when(j == 0)
    def _():
        h_ref[...] = _rmsnorm_rows(x_ref[...], g_ref[...]).astype(BF16)

    y = jnp.dot(h_ref[...], w_ref[...], preferred_element_type=F32)
    tn = y.shape[1]

    def epilogue(mode, gain_row, f32_copy):
        z = y
        if mode == "silu":
            z = _silu(y)
        elif mode in ("rope_q", "rope_k"):
            cos, sin = cos_ref[...], sin_ref[...]
            scale = RET_DK ** -0.5 if mode == "rope_k" else 1.0
            half = RET_DK // 2
            parts = []
            for hh in range(tn // RET_DK):
                x1 = y[:, hh * RET_DK: hh * RET_DK + half]
                x2 = y[:, hh * RET_DK + half: (hh + 1) * RET_DK]
                parts.append((x1 * cos - x2 * sin) * scale)
                parts.append((x1 * sin + x2 * cos) * scale)
            z = jnp.concatenate(parts, axis=1)
        elif mode == "headnorm":
            gh = gh_ref[gain_row:gain_row + 1, :]
            z = jnp.concatenate([_rmsnorm_rows(y[:, hh * ATT_HD:(hh + 1) * ATT_HD], gh)
                                 for hh in range(tn // ATT_HD)], axis=1)
        o_ref[...] = z.astype(o_ref.dtype)
        if f32_copy:
            o32_ref[...] = z

    j0 = 0
    for n_tiles, mode, gain_row, f32_copy in segments:
        in_segment = jnp.logical_and(j >= j0, j < j0 + n_tiles)
        pl.when(in_segment)(functools.partial(epilogue, mode, gain_row, f32_copy))
        j0 += n_tiles


def _proj(x, g, w, layer, segments, *, rope=None, gains=None):
    t, d = x.shape
    tm, tn = PROJ_TM, PROJ_TN
    n_tiles = sum(seg[0] for seg in segments)
    n32_tiles = sum(seg[0] for seg in segments if seg[3])
    j32 = n_tiles - n32_tiles
    in_specs = [
        pl.BlockSpec((tm, d), lambda i, j: (i, 0)),
        pl.BlockSpec((1, d), lambda i, j: (0, 0)),
        pl.BlockSpec((None, d, tn), lambda i, j: (layer, 0, j)),
    ]
    args = [x, g.reshape(1, d), w]
    if rope is not None:
        in_specs += [pl.BlockSpec((tm, RET_DK // 2), lambda i, j: (i, 0))] * 2
        args += list(rope)
    if gains is not None:
        in_specs.append(pl.BlockSpec(gains.shape, lambda i, j: (0, 0)))
        args.append(gains)
    out_specs = [pl.BlockSpec((tm, tn), lambda i, j: (i, j))]
    out_shape = [jax.ShapeDtypeStruct((t, n_tiles * tn), BF16)]
    if n32_tiles:
        out_specs.append(pl.BlockSpec((tm, tn), lambda i, j: (i, jnp.maximum(j - j32, 0))))
        out_shape.append(jax.ShapeDtypeStruct((t, n32_tiles * tn), F32))
    kern = functools.partial(_proj_kernel, segments=tuple(segments), has_rope=rope is not None,
                             has_gain=gains is not None, has_f32=bool(n32_tiles))
    return pl.pallas_call(
        kern,
        grid=(t // tm, n_tiles),
        in_specs=in_specs,
        out_specs=out_specs,
        out_shape=out_shape,
        scratch_shapes=[pltpu.VMEM((tm, d), BF16)],
        compiler_params=_params("arbitrary", "arbitrary"),
        name="proj",
    )(*args)


def _glu_kernel(x_ref, g_ref, wa_ref, wb_ref, o_ref, h_ref):
    @pl.when(pl.program_id(1) == 0)
    def _():
        h_ref[...] = _rmsnorm_rows(x_ref[...], g_ref[...]).astype(BF16)

    h = h_ref[...]
    a = jnp.dot(h, wa_ref[...], preferred_element_type=F32)
    b = jnp.dot(h, wb_ref[...], preferred_element_type=F32)
    o_ref[...] = a * jax.nn.sigmoid(b)


def _glu(x, g, w_in, layer):
    t, d = x.shape
    tm, tn = PROJ_TM, PROJ_TN
    nb = d // tn
    return pl.pallas_call(
        _glu_kernel,
        grid=(t // tm, nb),
        in_specs=[
            pl.BlockSpec((tm, d), lambda i, j: (i, 0)),
            pl.BlockSpec((1, d), lambda i, j: (0, 0)),
            pl.BlockSpec((None, d, tn), lambda i, j: (layer, 0, j)),
            pl.BlockSpec((None, d, tn), lambda i, j: (layer, 0, j + nb)),
        ],
        out_specs=pl.BlockSpec((tm, tn), lambda i, j: (i, j)),
        out_shape=jax.ShapeDtypeStruct((t, d), F32),
        scratch_shapes=[pltpu.VMEM((tm, d), BF16)],
        compiler_params=_params("parallel", "arbitrary"),
        name="conv_glu",
    )(x, g.reshape(1, d), w_in, w_in)


def _out_proj_kernel(x_ref, yp_ref, ys_ref, w_ref, o_ref, *, n_pt):
    i = pl.program_id(0)

    def body(y_ref):
        o_ref[...] = x_ref[...] + jnp.dot(y_ref[...], w_ref[...], preferred_element_type=F32)

    pl.when(i < n_pt)(lambda: body(yp_ref))
    pl.when(i >= n_pt)(lambda: body(ys_ref))


def _out_proj(x, y_p, y_s, w, layer):
    t, d = x.shape
    k = y_p.shape[1]
    tm, tn = PROJ_TM, PROJ_TN
    n_pt = y_p.shape[0] // tm
    _, prompt, sample = _row_tile_specs(tm, k, n_pt)
    return pl.pallas_call(
        functools.partial(_out_proj_kernel, n_pt=n_pt),
        grid=(t // tm, d // tn),
        in_specs=[
            pl.BlockSpec((tm, tn), lambda i, j: (i, j)),
            prompt,
            sample,
            pl.BlockSpec((None, k, tn), lambda i, j: (layer, 0, j)),
        ],
        out_specs=pl.BlockSpec((tm, tn), lambda i, j: (i, j)),
        out_shape=jax.ShapeDtypeStruct((t, d), F32),
        compiler_params=_params("parallel", "arbitrary"),
        name="out_proj",
    )(x, y_p, y_s, w)


def _conv_kernel(*refs, tt, carry):
    if carry:
        u_ref, w_ref, b_ref, gn_ref, z_ref, ext_ref, y_ref = refs
        halo_ref = None
    else:
        u_ref, halo_ref, w_ref, b_ref, gn_ref, z_ref, ext_ref, y_ref = refs
    d = u_ref.shape[-1]
    off = CONV_PAD - CONV_HALO
    rows = min(CONV_ROWS, tt)
    slab = rows + CONV_PAD

    if carry:
        @pl.when(pl.program_id(0) == 0)
        def _():
            ext_ref[0:CONV_PAD, :] = jnp.zeros((CONV_PAD, d), F32)
    else:
        ext_ref[0:SUBLANES, :] = jnp.zeros((SUBLANES, d), F32)
        ext_ref[off:CONV_PAD, :] = halo_ref[...]
    ext_ref[CONV_PAD:CONV_PAD + tt, :] = u_ref[...]

    def row_chunk(r, _):
        r0 = pl.multiple_of(r * rows, rows)
        for c0 in range(0, d, CONV_COLS):
            cols = slice(c0, c0 + CONV_COLS)
            x = ext_ref[pl.ds(r0, slab), cols]
            acc = jnp.zeros((rows, CONV_COLS), F32)
            for s in range(SUBLANES):
                xs = x if s == 0 else pltpu.roll(x, slab - s, 0)
                for a8 in range(0, CONV_PAD + SUBLANES, SUBLANES):
                    w = a8 + s - off
                    if 0 <= w < CONV_WIDTH:
                        acc = acc + xs[a8:a8 + rows, :] * w_ref[w:w + 1, cols]
            y_ref[pl.ds(r0, rows), cols] = acc + b_ref[:, cols]
        return 0

    lax.fori_loop(0, tt // rows, row_chunk, 0)

    z_ref[...] = _silu(_rmsnorm_rows(y_ref[...], gn_ref[...])).astype(z_ref.dtype)

    if carry:
        ext_ref[0:CONV_PAD, :] = ext_ref[tt:tt + CONV_PAD, :]


def _conv_prompt(u, n_rows, w_dw, b_dw, g_norm, layer):
    d = u.shape[1]
    tt = CONV_TT
    kern = functools.partial(_conv_kernel, tt=tt, carry=True)
    return pl.pallas_call(
        kern,
        grid=(n_rows // tt,),
        in_specs=[
            pl.BlockSpec((tt, d), lambda i: (i, 0)),
            pl.BlockSpec((None, CONV_WIDTH, d), lambda i: (layer, 0, 0)),
            pl.BlockSpec((1, d), lambda i: (0, 0)),
            pl.BlockSpec((1, d), lambda i: (0, 0)),
        ],
        out_specs=pl.BlockSpec((tt, d), lambda i: (i, 0)),
        out_shape=jax.ShapeDtypeStruct((n_rows, d), BF16),
        scratch_shapes=[pltpu.VMEM((CONV_PAD + tt, d), F32), pltpu.VMEM((tt, d), F32)],
        compiler_params=_params("arbitrary"),
        name="conv_prompt",
    )(u, w_dw, b_dw.reshape(1, d), g_norm.reshape(1, d))


def _conv_sample(u, row0, cache, w_dw, b_dw, g_norm, layer):
    _, nb, _, d = cache.shape
    tt = (u.shape[0] - row0) // nb
    rb0 = row0 // tt
    kern = functools.partial(_conv_kernel, tt=tt, carry=False)
    return pl.pallas_call(
        kern,
        grid=(nb,),
        in_specs=[
            pl.BlockSpec((tt, d), lambda b: (b + rb0, 0)),
            pl.BlockSpec((None, None, CONV_HALO, d), lambda b: (layer, b, 0, 0)),
            pl.BlockSpec((None, CONV_WIDTH, d), lambda b: (layer, 0, 0)),
            pl.BlockSpec((1, d), lambda b: (0, 0)),
            pl.BlockSpec((1, d), lambda b: (0, 0)),
        ],
        out_specs=pl.BlockSpec((tt, d), lambda b: (b, 0)),
        out_shape=jax.ShapeDtypeStruct((nb * tt, d), BF16),
        scratch_shapes=[pltpu.VMEM((CONV_PAD + tt, d), F32), pltpu.VMEM((tt, d), F32)],
        compiler_params=_params("parallel"),
        name="conv_sample",
    )(u, cache, w_dw, b_dw.reshape(1, d), g_norm.reshape(1, d))


def _ret_kernel(lg_ref, gl_ref, q_ref, k_ref, v_ref, sg_ref, s0_ref, go_ref, o_ref, sout_ref, s_ref, *, L):
    hg = pl.program_id(1)
    c = pl.program_id(2)

    @pl.when(c == 0)
    def _():
        s_ref[...] = s0_ref[...]

    ri = lax.broadcasted_iota(jnp.int32, (L, L), 0)
    ci = lax.broadcasted_iota(jnp.int32, (L, L), 1)
    diff = (ri - ci).astype(F32)
    row = lax.broadcasted_iota(jnp.int32, (L, 1), 0).astype(F32)
    for hh in range(RET_HP):
        lg = lg_ref[hg * RET_HP + hh]
        g_chunk = gl_ref[hg * RET_HP + hh]
        kcols = slice(hh * RET_DK, (hh + 1) * RET_DK)
        vcols = slice(hh * RET_DV, (hh + 1) * RET_DV)
        q, k, v = q_ref[:, kcols], k_ref[:, kcols], v_ref[:, vcols]
        decay = jnp.where(diff >= 0, jnp.exp(lg * jnp.maximum(diff, 0.0)), 0.0)
        inner = lax.dot_general(q, k, (((1,), (1,)), ((), ())), preferred_element_type=F32) * decay
        o = jnp.dot(inner.astype(BF16), v, preferred_element_type=F32)

        state = s_ref[hh]
        q_dec = jnp.exp(lg * (row + 1.0))
        o = o + jnp.dot(q, state.astype(BF16), preferred_element_type=F32) * q_dec
        k_dec = jnp.exp(lg * (L - 1.0 - row))
        kd = (k.astype(F32) * k_dec).astype(BF16)
        new_state = state * g_chunk + lax.dot_general(kd, v, (((0,), (0,)), ((), ())),
                                                      preferred_element_type=F32)
        s_ref[hh] = new_state

        o = _rmsnorm_rows(o, go_ref[:, vcols])
        o_ref[:, vcols] = (sg_ref[:, vcols].astype(F32) * o).astype(o_ref.dtype)

    @pl.when(c == pl.num_programs(2) - 1)
    def _():
        sout_ref[...] = s_ref[...]


def _retention(qkvg, s0, layer, g_out, row0, n_seq, seq_len, L, decay_consts):
    nc = seq_len // L
    rb0 = row0 // L
    hp, dk, dv = RET_HP, RET_DK, RET_DV
    ng = RET_HEADS // hp
    row_map = lambda b, g, c: b * nc + c + rb0
    kern = functools.partial(_ret_kernel, L=L)
    lg, gl = decay_consts
    v0 = 2 * ng * dk // dv
    o, s_out = pl.pallas_call(
        kern,
        grid=(n_seq, ng, nc),
        in_specs=[
            pl.BlockSpec(memory_space=pltpu.SMEM),
            pl.BlockSpec(memory_space=pltpu.SMEM),
            pl.BlockSpec((L, hp * dk), lambda b, g, c: (row_map(b, g, c), g)),
            pl.BlockSpec((L, hp * dk), lambda b, g, c: (row_map(b, g, c), g + ng)),
            pl.BlockSpec((L, hp * dv), lambda b, g, c: (row_map(b, g, c), g + v0)),
            pl.BlockSpec((L, hp * dv), lambda b, g, c: (row_map(b, g, c), g + v0 + ng)),
            pl.BlockSpec((None, None, hp, dk, dv), lambda b, g, c: (layer, b, g, 0, 0)),
            pl.BlockSpec((1, hp * dv), lambda b, g, c: (0, g)),
        ],
        out_specs=[
            pl.BlockSpec((L, hp * dv), lambda b, g, c: (b * nc + c, g)),
            pl.BlockSpec((None, hp, dk, dv), lambda b, g, c: (b, g, 0, 0)),
        ],
        out_shape=[
            jax.ShapeDtypeStruct((n_seq * seq_len, RET_HEADS * dv), BF16),
            jax.ShapeDtypeStruct((n_seq, RET_HEADS, dk, dv), F32),
        ],
        scratch_shapes=[pltpu.VMEM((hp, dk, dv), F32)],
        compiler_params=_params("parallel", "parallel", "arbitrary"),
        name="retention",
    )(lg, gl, qkvg, qkvg, qkvg, qkvg, s0, g_out.reshape(1, RET_HEADS * dv))
    return o, s_out


def _softmax_weights(s):
    m = jnp.max(s, axis=-1, keepdims=True)
    p = jnp.exp(s - m)
    return p, jnp.sum(p, axis=-1, keepdims=True)


def _att_prompt_kernel(q_ref, k0_ref, k1_ref, k2_ref, v0_ref, v1_ref, v2_ref, bias_ref, o_ref):
    qb = pl.program_id(1)
    k_pos = qb * ATT_QB - BAND_PAST + lax.broadcasted_iota(jnp.int32, (1, ATT_KW), 1)
    pos_mask = jnp.where(k_pos >= 0, 0.0, MASK_VALUE).astype(F32)
    for hh in range(ATT_HP):
        cols = slice(hh * ATT_HD, (hh + 1) * ATT_HD)
        q = q_ref[:, cols]
        k = jnp.concatenate([k0_ref[:, cols], k1_ref[:, cols], k2_ref[:, cols]], axis=0)
        v = jnp.concatenate([v0_ref[:, cols], v1_ref[:, cols], v2_ref[:, cols]], axis=0)
        s = lax.dot_general(q, k, (((1,), (1,)), ((), ())), preferred_element_type=F32)
        s = s * (ATT_HD ** -0.5) + bias_ref[hh] + pos_mask
        p, l = _softmax_weights(s)
        o = jnp.dot(p.astype(BF16), v, preferred_element_type=F32)
        o_ref[:, cols] = (o / l).astype(o_ref.dtype)


def _att_prompt(qkv, bias, n_rows):
    nqb = n_rows // ATT_QB
    nw = ATT_KW // ATT_QB
    ng = ATT_HEADS // ATT_HP
    blk = (ATT_QB, ATT_HP * ATT_HD)

    def kv_spec(back, section):
        return pl.BlockSpec(blk, lambda g, i: (jnp.maximum(i - back, 0), g + section * ng))

    k_specs = [kv_spec(nw - 1 - n, 1) for n in range(nw)]
    v_specs = [kv_spec(nw - 1 - n, 2) for n in range(nw)]
    return pl.pallas_call(
        _att_prompt_kernel,
        grid=(ng, nqb),
        in_specs=[pl.BlockSpec(blk, lambda g, i: (i, g))] + k_specs + v_specs + [
            pl.BlockSpec((ATT_HP, ATT_QB, ATT_KW), lambda g, i: (g, 0, 0)),
        ],
        out_specs=pl.BlockSpec(blk, lambda g, i: (i, g)),
        out_shape=jax.ShapeDtypeStruct((n_rows, ATT_HEADS * ATT_HD), BF16),
        compiler_params=_params("parallel", "arbitrary"),
        name="att_prompt",
    )(*([qkv] * (1 + 2 * nw)), bias)


def _att_sample_kernel(q_ref, kn_ref, vn_ref, kc_ref, vc_ref, bc_ref, bn_ref, o_ref):
    scale = ATT_HD ** -0.5
    dims = (((1,), (1,)), ((), ()))
    for h in range(ATT_HEADS):
        cols = slice(h * ATT_HD, (h + 1) * ATT_HD)
        q = q_ref[:, cols]
        lc = kc_ref.shape[0] // ATT_HEADS
        kc = kc_ref[pl.ds(h, lc, stride=ATT_HEADS), :].astype(BF16)
        vc = vc_ref[pl.ds(h, lc, stride=ATT_HEADS), :].astype(BF16)
        sc = lax.dot_general(q, kc, dims, preferred_element_type=F32) * scale + bc_ref[h]
        sn = lax.dot_general(q, kn_ref[:, cols], dims, preferred_element_type=F32) * scale + bn_ref[h]
        m = jnp.maximum(jnp.max(sc, axis=-1, keepdims=True), jnp.max(sn, axis=-1, keepdims=True))
        pc = jnp.exp(sc - m)
        pn = jnp.exp(sn - m)
        l = jnp.sum(pc, axis=-1, keepdims=True) + jnp.sum(pn, axis=-1, keepdims=True)
        o = jnp.dot(pc.astype(BF16), vc, preferred_element_type=F32)
        o = o + jnp.dot(pn.astype(BF16), vn_ref[:, cols], preferred_element_type=F32)
        o_ref[:, cols] = (o / l).astype(o_ref.dtype)


def _att_sample(qkv, row0, cache_k, cache_v, layer, bias_c, bias_n):
    _, nb, lc, nh, hd = cache_k.shape
    d = nh * hd
    ts = (qkv.shape[0] - row0) // nb
    rb0 = row0 // ts
    new_specs = [pl.BlockSpec((ts, d), lambda b, section=section: (b + rb0, section)) for section in range(3)]
    cache_k = cache_k.reshape(cache_k.shape[0], nb, lc * nh, hd)
    cache_v = cache_v.reshape(cache_v.shape[0], nb, lc * nh, hd)
    cache_spec = pl.BlockSpec((None, None, lc * nh, hd), lambda b: (layer, b, 0, 0))
    return pl.pallas_call(
        _att_sample_kernel,
        grid=(nb,),
        in_specs=new_specs + [cache_spec, cache_spec,
                              pl.BlockSpec((nh, ts, lc), lambda b: (0, 0, 0)),
                              pl.BlockSpec((nh, ts, ts), lambda b: (0, 0, 0))],
        out_specs=pl.BlockSpec((ts, d), lambda b: (b, 0)),
        out_shape=jax.ShapeDtypeStruct((nb * ts, d), BF16),
        compiler_params=_params("parallel"),
        name="att_sample",
    )(qkv, qkv, qkv, cache_k, cache_v, bias_c, bias_n)


def _band_bias(rel_table, nq, nk, chunk_band):
    nh = rel_table.shape[0]
    n = nq + nk - 1
    q_minus_k = nq - 1 - jnp.arange(n, dtype=jnp.int32) + BAND_PAST
    r = rel_table[:, jnp.clip(q_minus_k, -REL_CLIP, REL_CLIP) + REL_CLIP].astype(F32)
    r = jnp.pad(r, ((0, 0), (0, 1)))
    skew = jnp.tile(r, (1, nq))[:, :nq * n].reshape(nh, nq, n)
    bias = skew[:, :, nq - 1:nq - 1 + nk]
    if chunk_band:
        qc = jnp.arange(nq, dtype=jnp.int32)[:, None] // CHUNK
        kc = jnp.arange(nk, dtype=jnp.int32)[None, :] // CHUNK
        visible = (kc >= qc) & (kc <= qc + N_PREV_CHUNKS)
        bias = jnp.where(visible[None], bias, MASK_VALUE)
    return bias


def kernel(x_prompt, x_sample, cache_conv, state_ret, cache_att_k, cache_att_v, g_ffn1, w_ffn1_gu, w_ffn1_down, g_mix, g_ffn2, w_ffn2_gu, w_ffn2_down, w_conv_in, w_conv_dw, b_conv_dw, g_conv_norm, w_conv_out, w_ret_in, g_ret_out, w_ret_out, w_att_qkv, g_att_q, g_att_k, att_rel_bias, w_att_out):
    bp, tp, d = x_prompt.shape
    bs, ts, _ = x_sample.shape
    depth = g_mix.shape[0]
    n_p = bp * tp
    n_s = bs * ts
    assert bp == 1, "the prompt path carries conv / retention state along one sequence"

    w_ffn1_gu, w_ffn1_down, w_ffn2_gu, w_ffn2_down, w_conv_in, w_conv_out, w_ret_in, w_ret_out, w_att_qkv, w_att_out = (
        w.astype(BF16) for w in (w_ffn1_gu, w_ffn1_down, w_ffn2_gu, w_ffn2_down, w_conv_in, w_conv_out,
                                 w_ret_in, w_ret_out, w_att_qkv, w_att_out))

    pos = jnp.concatenate([jnp.arange(tp, dtype=jnp.int32),
                           jnp.tile(PAST_LEN + jnp.arange(ts, dtype=jnp.int32), bs)])
    half = RET_DK // 2
    freq = ROPE_BASE ** (-jnp.arange(half, dtype=F32) / half)
    ang = pos.astype(F32)[:, None] * freq[None, :]
    rope = (jnp.cos(ang), jnp.sin(ang))
    log_g = jnp.log1p(-jnp.exp2(-5.0 - jnp.arange(RET_HEADS, dtype=F32)))
    l_p = RET_L if tp % RET_L == 0 else tp
    l_s = ts

    conv_p, conv_s, ret_p, ret_s = [], [], [], []
    kp_l, vp_l, ks_l, vs_l = [], [], [], []
    xs = [x_prompt.reshape(n_p, d), x_sample.reshape(n_s, d)]
    for i in range(depth):
        kind, j = i % N_MIXERS, i // N_MIXERS
        x = _ffn(xs, g_ffn1[i], w_ffn1_gu, w_ffn1_down, i, n_p)
        if kind == 0:
            u = _glu(x, g_mix[i], w_conv_in, j)
            z_p = _conv_prompt(u, n_p, w_conv_dw, b_conv_dw[j], g_conv_norm[j], j)
            z_s = _conv_sample(u, n_p, cache_conv, w_conv_dw, b_conv_dw[j], g_conv_norm[j], j)
            x = _out_proj(x, z_p, z_s, w_conv_out, j)
            conv_p.append(u[n_p - CONV_HALO:n_p].reshape(bp, CONV_HALO, d))
            conv_s.append(u[n_p:].reshape(bs, ts, d)[:, ts - CONV_HALO:])
        elif kind == 1:
            nq_t = RET_HEADS * RET_DK // PROJ_TN
            nv_t = RET_HEADS * RET_DV // PROJ_TN
            segments = [(nq_t, "rope_q", 0, False), (nq_t, "rope_k", 0, False),
                        (nv_t, "plain", 0, False), (nv_t, "silu", 0, False)]
            (qkvg,) = _proj(x, g_mix[i], w_ret_in, j, segments, rope=rope)
            zero_state = jnp.zeros((1, bp, RET_HEADS, RET_DK, RET_DV), F32)
            o_p, s_p = _retention(qkvg, zero_state, 0, g_ret_out[j], 0, bp, tp, l_p,
                                  (log_g, jnp.exp(log_g * l_p)))
            o_s, s_s = _retention(qkvg, state_ret, j, g_ret_out[j], n_p, bs, ts, l_s,
                                  (log_g, jnp.exp(log_g * l_s)))
            x = _out_proj(x, o_p, o_s, w_ret_out, j)
            ret_p.append(s_p)
            ret_s.append(s_s)
        else:
            nd_t = d // PROJ_TN
            segments = [(nd_t, "headnorm", 0, False), (nd_t, "headnorm", 1, True), (nd_t, "plain", 0, True)]
            qkv, kv32 = _proj(x, g_mix[i], w_att_qkv, j, segments, gains=jnp.stack([g_att_q[j], g_att_k[j]]))
            k32, v32 = kv32[:, :d], kv32[:, d:]
            lc = cache_att_k.shape[2]
            o_p = _att_prompt(qkv, _band_bias(att_rel_bias[j], ATT_QB, ATT_KW, True), n_p)
            bias_s = _band_bias(att_rel_bias[j], ts, lc + ts, False)
            o_s = _att_sample(qkv, n_p, cache_att_k, cache_att_v, j, bias_s[:, :, :lc], bias_s[:, :, lc:])
            x = _out_proj(x, o_p, o_s, w_att_out, j)
            keep = min(BAND_PAST, tp)
            kp_l.append(k32[n_p - keep:n_p].reshape(bp, keep, ATT_HEADS, ATT_HD))
            vp_l.append(v32[n_p - keep:n_p].reshape(bp, keep, ATT_HEADS, ATT_HD))
            ks_l.append(k32[n_p:].reshape(bs, ts, ATT_HEADS, ATT_HD))
            vs_l.append(v32[n_p:].reshape(bs, ts, ATT_HEADS, ATT_HD))
        xs = _ffn([x], g_ffn2[i], w_ffn2_gu, w_ffn2_down, i, n_p, split_out=(i == depth - 1))
        if i < depth - 1:
            xs = [xs]

    return (xs[0].reshape(bp, tp, d), xs[1].reshape(bs, ts, d),
            jnp.stack(conv_p), jnp.stack(conv_s),
            jnp.stack(ret_p), jnp.stack(ret_s),
            jnp.stack(kp_l), jnp.stack(vp_l),
            jnp.stack(ks_l), jnp.stack(vs_l))
```

```python
import functools

import jax
import jax.numpy as jnp
from jax import lax
from jax.experimental import pallas as pl
from jax.experimental.pallas import tpu as pltpu

F32 = jnp.float32
BF16 = jnp.bfloat16

CHUNK = 64
NORM_EPS = 1e-6
CONV_WIDTH = 31
CONV_HALO = CONV_WIDTH - 1
RET_HEADS = 8
RET_DK = 256
RET_DV = 512
ROPE_BASE = 10000.0
ATT_HEADS = 16
ATT_HD = 128
N_PREV_CHUNKS = 8
BAND_PAST = N_PREV_CHUNKS * CHUNK
REL_CLIP = 256
PAST_LEN = 4096
N_MIXERS = 3

LANES = 128
SUBLANES = 8
VMEM_BYTES = 64 * 1024 * 1024
VMEM_LIMIT_BYTES = 56 * 1024 * 1024
FFN_VMEM_LIMIT_BYTES = 60 * 1024 * 1024
assert FFN_VMEM_LIMIT_BYTES < VMEM_BYTES
MASK_VALUE = -1e30

FFN_TM = 1024
FFN_TM_SPLIT = 512
FFN_TF = 512
FFN_CAST_BLOCKS = 8
PROJ_TM = 1024
PROJ_TN = 1024
PROJ_SUB = 512
GLU_TN = 512
CONV_TT = 256
CONV_ROWS = 64
CONV_COLS = 128
CONV_PAD = 32
RET_L = 256
RET_HP = 2
ATT_QB = 256
ATT_KW = ATT_QB + BAND_PAST
ATT_HP = 4


def _params(*sem, vmem_limit_bytes=VMEM_LIMIT_BYTES):
    return pltpu.CompilerParams(dimension_semantics=sem, vmem_limit_bytes=vmem_limit_bytes)


def _as_bf16(w):
    return w if w.dtype == BF16 else w.astype(BF16)


def _rmsnorm_rows(x, g):
    ms = jnp.mean(x * x, axis=-1, keepdims=True)
    return x * lax.rsqrt(ms + NORM_EPS) * g


def _silu(x):
    return x * jax.nn.sigmoid(x)


def _row_tile_specs(tm, width, n_pt):
    whole = pl.BlockSpec((tm, width), lambda i, j: (i, 0))
    prompt = pl.BlockSpec((tm, width), lambda i, j: (jnp.minimum(i, n_pt - 1), 0))
    sample = pl.BlockSpec((tm, width), lambda i, j: (jnp.maximum(i - n_pt, 0), 0))
    return whole, prompt, sample


def _ffn_kernel(*refs, n_pt, split_in, split_out, cast_next):
    refs = list(refs)
    x_refs = [refs.pop(0) for _ in range(2 if split_in else 1)]
    g_ref, wg_ref, wu_ref, wd_ref = refs[:4]
    refs = refs[4:]
    h_ref = refs.pop()
    if cast_next:
        next_gu_ref, next_down_ref = refs[:2]
        cast_gu_ref, cast_down_ref = refs[-2:]
        refs = refs[2:-2]
    o_refs = refs
    i = pl.program_id(0)
    j = pl.program_id(1)

    def body(x_ref, o_ref):
        @pl.when(j == 0)
        def _():
            x = x_ref[...]
            h_ref[...] = _rmsnorm_rows(x, g_ref[...]).astype(BF16)
            o_ref[...] = x

        h = h_ref[...]
        gate = jnp.dot(h, wg_ref[...], preferred_element_type=F32)
        up = jnp.dot(h, wu_ref[...], preferred_element_type=F32)
        a = (_silu(gate) * (0.5 * up)).astype(BF16)
        o_ref[...] += jnp.dot(a, wd_ref[...], preferred_element_type=F32)

    if not (split_in or split_out):
        body(x_refs[0], o_refs[0])
    else:
        pl.when(i < n_pt)(lambda: body(x_refs[0], o_refs[0]))
        pl.when(i >= n_pt)(lambda: body(x_refs[-1], o_refs[-1]))

    if cast_next:
        @pl.when(i < FFN_CAST_BLOCKS)
        def _():
            cast_gu_ref[...] = next_gu_ref[...].astype(BF16)
            cast_down_ref[...] = next_down_ref[...].astype(BF16)


def _ffn(xs, g, w_gu, w_down, n_p, split_out=False, next_weights=None):
    split_in = len(xs) == 2
    d = xs[0].shape[1]
    t = sum(x.shape[0] for x in xs)
    f = w_down.shape[0]
    nf = f // FFN_TF
    tm = FFN_TM_SPLIT if (split_in or split_out) else FFN_TM
    whole, prompt, sample = _row_tile_specs(tm, d, n_p // tm)
    in_specs = ([prompt, sample] if split_in else [whole]) + [
        pl.BlockSpec((1, d), lambda i, j: (0, 0)),
        pl.BlockSpec((d, FFN_TF), lambda i, j: (0, j)),
        pl.BlockSpec((d, FFN_TF), lambda i, j: (0, j + nf)),
        pl.BlockSpec((FFN_TF, d), lambda i, j: (j, 0)),
    ]
    args = list(xs) + [g.reshape(1, d), w_gu, w_gu, w_down]
    out_specs = [prompt, sample] if split_out else [whole]
    out_shape = ([jax.ShapeDtypeStruct((n_p, d), F32), jax.ShapeDtypeStruct((t - n_p, d), F32)] if split_out
                 else [jax.ShapeDtypeStruct((t, d), F32)])
    if next_weights is not None:
        gu_stack, down_stack, layer = next_weights
        nb = FFN_CAST_BLOCKS
        assert t // tm >= nb
        bi = lambda i, j: jnp.where(i < nb, i, nb - 1)
        bj = lambda i, j: jnp.where(i < nb, j, nf - 1)
        gu_blk, down_blk = (d // nb, 2 * f // nf), (f // nf, d // nb)
        in_specs += [pl.BlockSpec((None,) + gu_blk, lambda i, j: (layer, bi(i, j), bj(i, j))),
                     pl.BlockSpec((None,) + down_blk, lambda i, j: (layer, bj(i, j), bi(i, j)))]
        args += [gu_stack, down_stack]
        out_specs += [pl.BlockSpec(gu_blk, lambda i, j: (bi(i, j), bj(i, j))),
                      pl.BlockSpec(down_blk, lambda i, j: (bj(i, j), bi(i, j)))]
        out_shape += [jax.ShapeDtypeStruct((d, 2 * f), BF16), jax.ShapeDtypeStruct((f, d), BF16)]
    kern = functools.partial(_ffn_kernel, n_pt=n_p // tm, split_in=split_in, split_out=split_out,
                             cast_next=next_weights is not None)
    return pl.pallas_call(
        kern,
        grid=(t // tm, nf),
        in_specs=in_specs,
        out_specs=out_specs,
        out_shape=out_shape,
        scratch_shapes=[pltpu.VMEM((tm, d), BF16)],
        compiler_params=_params("arbitrary", "arbitrary", vmem_limit_bytes=FFN_VMEM_LIMIT_BYTES),
        name="ffn",
    )(*args)


def _proj_kernel(*refs, segments, has_rope, has_gain, has_f32):
    it = iter(refs)
    x_ref, g_ref, w_ref = next(it), next(it), next(it)
    cos_ref = sin_ref = gh_ref = o32_ref = None
    if has_rope:
        cos_ref, sin_ref = next(it), next(it)
    if has_gain:
        gh_ref = next(it)
    o_ref = next(it)
    if has_f32:
        o32_ref = next(it)
    h_ref = next(it)
    j = pl.program_id(1)

    @pl.when(j == 0)
    def _():
        h_ref[...] = _rmsnorm_rows(x_ref[...], g_ref[...]).astype(BF16)

    def epilogue(y, mode, gain_row):
        if mode == "silu":
            return _silu(y)
        if mode in ("rope_q", "rope_k"):
            cos, sin = cos_ref[...], sin_ref[...]
            scale = RET_DK ** -0.5 if mode == "rope_k" else 1.0
            half = RET_DK // 2
            parts = []
            for hh in range(y.shape[1] // RET_DK):
                x1 = y[:, hh * RET_DK: hh * RET_DK + half]
                x2 = y[:, hh * RET_DK + half: (hh + 1) * RET_DK]
                parts.append((x1 * cos - x2 * sin) * scale)
                parts.append((x1 * sin + x2 * cos) * scale)
            return jnp.concatenate(parts, axis=1)
        if mode == "headnorm":
            gh = gh_ref[gain_row:gain_row + 1, :]
            return jnp.concatenate([_rmsnorm_rows(y[:, hh * ATT_HD:(hh + 1) * ATT_HD], gh)
                                    for hh in range(y.shape[1] // ATT_HD)], axis=1)
        return y

    def column_tile(mode, gain_row, f32_copy):
        for c0 in range(0, w_ref.shape[1], PROJ_SUB):
            cols = slice(c0, c0 + PROJ_SUB)
            y = jnp.dot(h_ref[...], _as_bf16(w_ref[:, cols]), preferred_element_type=F32)
            z = epilogue(y, mode, gain_row)
            o_ref[:, cols] = z.astype(o_ref.dtype)
            if f32_copy:
                o32_ref[:, cols] = z

    j0 = 0
    for n_tiles, mode, gain_row, f32_copy in segments:
        in_segment = jnp.logical_and(j >= j0, j < j0 + n_tiles)
        pl.when(in_segment)(functools.partial(column_tile, mode, gain_row, f32_copy))
        j0 += n_tiles


def _proj(x, g, w, layer, segments, *, rope=None, gains=None):
    t, d = x.shape
    tm, tn = PROJ_TM, PROJ_TN
    n_tiles = sum(seg[0] for seg in segments)
    n32_tiles = sum(seg[0] for seg in segments if seg[3])
    j32 = n_tiles - n32_tiles
    in_specs = [
        pl.BlockSpec((tm, d), lambda i, j: (i, 0)),
        pl.BlockSpec((1, d), lambda i, j: (0, 0)),
        pl.BlockSpec((None, d, tn), lambda i, j: (layer, 0, j)),
    ]
    args = [x, g.reshape(1, d), w]
    if rope is not None:
        in_specs += [pl.BlockSpec((tm, RET_DK // 2), lambda i, j: (i, 0))] * 2
        args += list(rope)
    if gains is not None:
        in_specs.append(pl.BlockSpec(gains.shape, lambda i, j: (0, 0)))
        args.append(gains)
    out_specs = [pl.BlockSpec((tm, tn), lambda i, j: (i, j))]
    out_shape = [jax.ShapeDtypeStruct((t, n_tiles * tn), BF16)]
    if n32_tiles:
        out_specs.append(pl.BlockSpec((tm, tn), lambda i, j: (i, jnp.maximum(j - j32, 0))))
        out_shape.append(jax.ShapeDtypeStruct((t, n32_tiles * tn), F32))
    kern = functools.partial(_proj_kernel, segments=tuple(segments), has_rope=rope is not None,
                             has_gain=gains is not None, has_f32=bool(n32_tiles))
    return pl.pallas_call(
        kern,
        grid=(t // tm, n_tiles),
        in_specs=in_specs,
        out_specs=out_specs,
        out_shape=out_shape,
        scratch_shapes=[pltpu.VMEM((tm, d), BF16)],
        compiler_params=_params("arbitrary", "arbitrary"),
        name="proj",
    )(*args)


def _glu_kernel(x_ref, g_ref, wa_ref, wb_ref, o_ref, h_ref):
    @pl.when(pl.program_id(1) == 0)
    def _():
        h_ref[...] = _rmsnorm_rows(x_ref[...], g_ref[...]).astype(BF16)

    h = h_ref[...]
    a = jnp.dot(h, _as_bf16(wa_ref[...]), preferred_element_type=F32)
    b = jnp.dot(h, _as_bf16(wb_ref[...]), preferred_element_type=F32)
    o_ref[...] = a * jax.nn.sigmoid(b)


def _glu(x, g, w_in, layer):
    t, d = x.shape
    tm, tn = PROJ_TM, GLU_TN
    nb = d // tn
    return pl.pallas_call(
        _glu_kernel,
        grid=(t // tm, nb),
        in_specs=[
            pl.BlockSpec((tm, d), lambda i, j: (i, 0)),
            pl.BlockSpec((1, d), lambda i, j: (0, 0)),
            pl.BlockSpec((None, d, tn), lambda i, j: (layer, 0, j)),
            pl.BlockSpec((None, d, tn), lambda i, j: (layer, 0, j + nb)),
        ],
        out_specs=pl.BlockSpec((tm, tn), lambda i, j: (i, j)),
        out_shape=jax.ShapeDtypeStruct((t, d), F32),
        scratch_shapes=[pltpu.VMEM((tm, d), BF16)],
        compiler_params=_params("parallel", "arbitrary"),
        name="conv_glu",
    )(x, g.reshape(1, d), w_in, w_in)


def _out_proj_kernel(x_ref, yp_ref, ys_ref, w_ref, o_ref, *, n_pt):
    i = pl.program_id(0)

    def body(y_ref):
        for c0 in range(0, w_ref.shape[1], PROJ_SUB):
            cols = slice(c0, c0 + PROJ_SUB)
            o_ref[:, cols] = x_ref[:, cols] + jnp.dot(y_ref[...], _as_bf16(w_ref[:, cols]),
                                                      preferred_element_type=F32)

    pl.when(i < n_pt)(lambda: body(yp_ref))
    pl.when(i >= n_pt)(lambda: body(ys_ref))


def _out_proj(x, y_p, y_s, w, layer, tn):
    t, d = x.shape
    k = y_p.shape[1]
    tm = PROJ_TM
    n_pt = y_p.shape[0] // tm
    _, prompt, sample = _row_tile_specs(tm, k, n_pt)
    return pl.pallas_call(
        functools.partial(_out_proj_kernel, n_pt=n_pt),
        grid=(t // tm, d // tn),
        in_specs=[
            pl.BlockSpec((tm, tn), lambda i, j: (i, j)),
            prompt,
            sample,
            pl.BlockSpec((None, k, tn), lambda i, j: (layer, 0, j)),
        ],
        out_specs=pl.BlockSpec((tm, tn), lambda i, j: (i, j)),
        out_shape=jax.ShapeDtypeStruct((t, d), F32),
        compiler_params=_params("parallel", "arbitrary"),
        name="out_proj",
    )(x, y_p, y_s, w)


def _conv_kernel(*refs, tt, carry):
    if carry:
        u_ref, w_ref, b_ref, gn_ref, z_ref, ext_ref, y_ref = refs
        halo_ref = None
    else:
        u_ref, halo_ref, w_ref, b_ref, gn_ref, z_ref, ext_ref, y_ref = refs
    d = u_ref.shape[-1]
    off = CONV_PAD - CONV_HALO
    rows = min(CONV_ROWS, tt)
    slab = rows + CONV_PAD

    if carry:
        @pl.when(pl.program_id(0) == 0)
        def _():
            ext_ref[0:CONV_PAD, :] = jnp.zeros((CONV_PAD, d), F32)
    else:
        ext_ref[0:SUBLANES, :] = jnp.zeros((SUBLANES, d), F32)
        ext_ref[off:CONV_PAD, :] = halo_ref[...]
    ext_ref[CONV_PAD:CONV_PAD + tt, :] = u_ref[...]

    def row_chunk(r, _):
        r0 = pl.multiple_of(r * rows, rows)
        for c0 in range(0, d, CONV_COLS):
            cols = slice(c0, c0 + CONV_COLS)
            x = ext_ref[pl.ds(r0, slab), cols]
            acc = jnp.zeros((rows, CONV_COLS), F32)
            for s in range(SUBLANES):
                xs = x if s == 0 else pltpu.roll(x, slab - s, 0)
                for a8 in range(0, CONV_PAD + SUBLANES, SUBLANES):
                    w = a8 + s - off
                    if 0 <= w < CONV_WIDTH:
                        acc = acc + xs[a8:a8 + rows, :] * w_ref[w:w + 1, cols]
            y_ref[pl.ds(r0, rows), cols] = acc + b_ref[:, cols]
        return 0

    lax.fori_loop(0, tt // rows, row_chunk, 0)

    z_ref[...] = _silu(_rmsnorm_rows(y_ref[...], gn_ref[...])).astype(z_ref.dtype)

    if carry:
        ext_ref[0:CONV_PAD, :] = ext_ref[tt:tt + CONV_PAD, :]


def _conv_prompt(u, n_rows, w_dw, b_dw, g_norm, layer):
    d = u.shape[1]
    tt = CONV_TT
    kern = functools.partial(_conv_kernel, tt=tt, carry=True)
    return pl.pallas_call(
        kern,
        grid=(n_rows // tt,),
        in_specs=[
            pl.BlockSpec((tt, d), lambda i: (i, 0)),
            pl.BlockSpec((None, CONV_WIDTH, d), lambda i: (layer, 0, 0)),
            pl.BlockSpec((1, d), lambda i: (0, 0)),
            pl.BlockSpec((1, d), lambda i: (0, 0)),
        ],
        out_specs=pl.BlockSpec((tt, d), lambda i: (i, 0)),
        out_shape=jax.ShapeDtypeStruct((n_rows, d), BF16),
        scratch_shapes=[pltpu.VMEM((CONV_PAD + tt, d), F32), pltpu.VMEM((tt, d), F32)],
        compiler_params=_params("arbitrary"),
        name="conv_prompt",
    )(u, w_dw, b_dw.reshape(1, d), g_norm.reshape(1, d))


def _conv_sample(u, row0, cache, w_dw, b_dw, g_norm, layer):
    _, nb, _, d = cache.shape
    tt = (u.shape[0] - row0) // nb
    rb0 = row0 // tt
    kern = functools.partial(_conv_kernel, tt=tt, carry=False)
    return pl.pallas_call(
        kern,
        grid=(nb,),
        in_specs=[
            pl.BlockSpec((tt, d), lambda b: (b + rb0, 0)),
            pl.BlockSpec((None, None, CONV_HALO, d), lambda b: (layer, b, 0, 0)),
            pl.BlockSpec((None, CONV_WIDTH, d), lambda b: (layer, 0, 0)),
            pl.BlockSpec((1, d), lambda b: (0, 0)),
            pl.BlockSpec((1, d), lambda b: (0, 0)),
        ],
        out_specs=pl.BlockSpec((tt, d), lambda b: (b, 0)),
        out_shape=jax.ShapeDtypeStruct((nb * tt, d), BF16),
        scratch_shapes=[pltpu.VMEM((CONV_PAD + tt, d), F32), pltpu.VMEM((tt, d), F32)],
        compiler_params=_params("parallel"),
        name="conv_sample",
    )(u, cache, w_dw, b_dw.reshape(1, d), g_norm.reshape(1, d))


def _ret_kernel(lg_ref, gl_ref, q_ref, k_ref, v_ref, sg_ref, s0_ref, go_ref, o_ref, sout_ref, s_ref, *, L):
    hg = pl.program_id(1)
    c = pl.program_id(2)

    @pl.when(c == 0)
    def _():
        s_ref[...] = s0_ref[...]

    ri = lax.broadcasted_iota(jnp.int32, (L, L), 0)
    ci = lax.broadcasted_iota(jnp.int32, (L, L), 1)
    diff = (ri - ci).astype(F32)
    row = lax.broadcasted_iota(jnp.int32, (L, 1), 0).astype(F32)
    for hh in range(RET_HP):
        lg = lg_ref[hg * RET_HP + hh]
        g_chunk = gl_ref[hg * RET_HP + hh]
        kcols = slice(hh * RET_DK, (hh + 1) * RET_DK)
        vcols = slice(hh * RET_DV, (hh + 1) * RET_DV)
        q, k, v = q_ref[:, kcols], k_ref[:, kcols], v_ref[:, vcols]
        decay = jnp.where(diff >= 0, jnp.exp(lg * jnp.maximum(diff, 0.0)), 0.0)
        inner = lax.dot_general(q, k, (((1,), (1,)), ((), ())), preferred_element_type=F32) * decay
        o = jnp.dot(inner.astype(BF16), v, preferred_element_type=F32)

        state = s_ref[hh]
        q_dec = jnp.exp(lg * (row + 1.0))
        o = o + jnp.dot(q, state.astype(BF16), preferred_element_type=F32) * q_dec
        k_dec = jnp.exp(lg * (L - 1.0 - row))
        kd = (k.astype(F32) * k_dec).astype(BF16)
        new_state = state * g_chunk + lax.dot_general(kd, v, (((0,), (0,)), ((), ())),
                                                      preferred_element_type=F32)
        s_ref[hh] = new_state

        o = _rmsnorm_rows(o, go_ref[:, vcols])
        o_ref[:, vcols] = (sg_ref[:, vcols].astype(F32) * o).astype(o_ref.dtype)

    @pl.when(c == pl.num_programs(2) - 1)
    def _():
        sout_ref[...] = s_ref[...]


def _retention(qkvg, s0, layer, g_out, row0, n_seq, seq_len, L, decay_consts):
    nc = seq_len // L
    rb0 = row0 // L
    hp, dk, dv = RET_HP, RET_DK, RET_DV
    ng = RET_HEADS // hp
    row_map = lambda b, g, c: b * nc + c + rb0
    kern = functools.partial(_ret_kernel, L=L)
    lg, gl = decay_consts
    v0 = 2 * ng * dk // dv
    o, s_out = pl.pallas_call(
        kern,
        grid=(n_seq, ng, nc),
        in_specs=[
            pl.BlockSpec(memory_space=pltpu.SMEM),
            pl.BlockSpec(memory_space=pltpu.SMEM),
            pl.BlockSpec((L, hp * dk), lambda b, g, c: (row_map(b, g, c), g)),
            pl.BlockSpec((L, hp * dk), lambda b, g, c: (row_map(b, g, c), g + ng)),
            pl.BlockSpec((L, hp * dv), lambda b, g, c: (row_map(b, g, c), g + v0)),
            pl.BlockSpec((L, hp * dv), lambda b, g, c: (row_map(b, g, c), g + v0 + ng)),
            pl.BlockSpec((None, None, hp, dk, dv), lambda b, g, c: (layer, b, g, 0, 0)),
            pl.BlockSpec((1, hp * dv), lambda b, g, c: (0, g)),
        ],
        out_specs=[
            pl.BlockSpec((L, hp * dv), lambda b, g, c: (b * nc + c, g)),
            pl.BlockSpec((None, hp, dk, dv), lambda b, g, c: (b, g, 0, 0)),
        ],
        out_shape=[
            jax.ShapeDtypeStruct((n_seq * seq_len, RET_HEADS * dv), BF16),
            jax.ShapeDtypeStruct((n_seq, RET_HEADS, dk, dv), F32),
        ],
        scratch_shapes=[pltpu.VMEM((hp, dk, dv), F32)],
        compiler_params=_params("parallel", "parallel", "arbitrary"),
        name="retention",
    )(lg, gl, qkvg, qkvg, qkvg, qkvg, s0, g_out.reshape(1, RET_HEADS * dv))
    return o, s_out


def _softmax_weights(s):
    m = jnp.max(s, axis=-1, keepdims=True)
    p = jnp.exp(s - m)
    return p, jnp.sum(p, axis=-1, keepdims=True)


def _att_prompt_kernel(q_ref, k0_ref, k1_ref, k2_ref, v0_ref, v1_ref, v2_ref, bias_ref, o_ref):
    qb = pl.program_id(1)
    k_pos = qb * ATT_QB - BAND_PAST + lax.broadcasted_iota(jnp.int32, (1, ATT_KW), 1)
    pos_mask = jnp.where(k_pos >= 0, 0.0, MASK_VALUE).astype(F32)
    for hh in range(ATT_HP):
        cols = slice(hh * ATT_HD, (hh + 1) * ATT_HD)
        q = q_ref[:, cols]
        k = jnp.concatenate([k0_ref[:, cols], k1_ref[:, cols], k2_ref[:, cols]], axis=0)
        v = jnp.concatenate([v0_ref[:, cols], v1_ref[:, cols], v2_ref[:, cols]], axis=0)
        s = lax.dot_general(q, k, (((1,), (1,)), ((), ())), preferred_element_type=F32)
        s = s * (ATT_HD ** -0.5) + bias_ref[hh] + pos_mask
        p, l = _softmax_weights(s)
        o = jnp.dot(p.astype(BF16), v, preferred_element_type=F32)
        o_ref[:, cols] = (o / l).astype(o_ref.dtype)


def _att_prompt(qkv, bias, n_rows):
    nqb = n_rows // ATT_QB
    nw = ATT_KW // ATT_QB
    ng = ATT_HEADS // ATT_HP
    blk = (ATT_QB, ATT_HP * ATT_HD)

    def kv_spec(back, section):
        return pl.BlockSpec(blk, lambda g, i: (jnp.maximum(i - back, 0), g + section * ng))

    k_specs = [kv_spec(nw - 1 - n, 1) for n in range(nw)]
    v_specs = [kv_spec(nw - 1 - n, 2) for n in range(nw)]
    return pl.pallas_call(
        _att_prompt_kernel,
        grid=(ng, nqb),
        in_specs=[pl.BlockSpec(blk, lambda g, i: (i, g))] + k_specs + v_specs + [
            pl.BlockSpec((ATT_HP, ATT_QB, ATT_KW), lambda g, i: (g, 0, 0)),
        ],
        out_specs=pl.BlockSpec(blk, lambda g, i: (i, g)),
        out_shape=jax.ShapeDtypeStruct((n_rows, ATT_HEADS * ATT_HD), BF16),
        compiler_params=_params("parallel", "arbitrary"),
        name="att_prompt",
    )(*([qkv] * (1 + 2 * nw)), bias)


def _att_sample_kernel(q_ref, kn_ref, vn_ref, kc_ref, vc_ref, bc_ref, bn_ref, o_ref):
    scale = ATT_HD ** -0.5
    dims = (((1,), (1,)), ((), ()))
    for h in range(ATT_HEADS):
        cols = slice(h * ATT_HD, (h + 1) * ATT_HD)
        q = q_ref[:, cols]
        lc = kc_ref.shape[0] // ATT_HEADS
        kc = kc_ref[pl.ds(h, lc, stride=ATT_HEADS), :].astype(BF16)
        vc = vc_ref[pl.ds(h, lc, stride=ATT_HEADS), :].astype(BF16)
        sc = lax.dot_general(q, kc, dims, preferred_element_type=F32) * scale + bc_ref[h]
        sn = lax.dot_general(q, kn_ref[:, cols], dims, preferred_element_type=F32) * scale + bn_ref[h]
        m = jnp.maximum(jnp.max(sc, axis=-1, keepdims=True), jnp.max(sn, axis=-1, keepdims=True))
        pc = jnp.exp(sc - m)
        pn = jnp.exp(sn - m)
        l = jnp.sum(pc, axis=-1, keepdims=True) + jnp.sum(pn, axis=-1, keepdims=True)
        o = jnp.dot(pc.astype(BF16), vc, preferred_element_type=F32)
        o = o + jnp.dot(pn.astype(BF16), vn_ref[:, cols], preferred_element_type=F32)
        o_ref[:, cols] = (o / l).astype(o_ref.dtype)


def _att_sample(qkv, row0, cache_k, cache_v, layer, bias_c, bias_n):
    _, nb, lc, nh, hd = cache_k.shape
    d = nh * hd
    ts = (qkv.shape[0] - row0) // nb
    rb0 = row0 // ts
    new_specs = [pl.BlockSpec((ts, d), lambda b, section=section: (b + rb0, section)) for section in range(3)]
    cache_k = cache_k.reshape(cache_k.shape[0], nb, lc * nh, hd)
    cache_v = cache_v.reshape(cache_v.shape[0], nb, lc * nh, hd)
    cache_spec = pl.BlockSpec((None, None, lc * nh, hd), lambda b: (layer, b, 0, 0))
    return pl.pallas_call(
        _att_sample_kernel,
        grid=(nb,),
        in_specs=new_specs + [cache_spec, cache_spec,
                              pl.BlockSpec((nh, ts, lc), lambda b: (0, 0, 0)),
                              pl.BlockSpec((nh, ts, ts), lambda b: (0, 0, 0))],
        out_specs=pl.BlockSpec((ts, d), lambda b: (b, 0)),
        out_shape=jax.ShapeDtypeStruct((nb * ts, d), BF16),
        compiler_params=_params("parallel"),
        name="att_sample",
    )(qkv, qkv, qkv, cache_k, cache_v, bias_c, bias_n)


def _band_bias(rel_table, nq, nk, chunk_band):
    nh = rel_table.shape[0]
    n = nq + nk - 1
    q_minus_k = nq - 1 - jnp.arange(n, dtype=jnp.int32) + BAND_PAST
    r = rel_table[:, jnp.clip(q_minus_k, -REL_CLIP, REL_CLIP) + REL_CLIP].astype(F32)
    r = jnp.pad(r, ((0, 0), (0, 1)))
    skew = jnp.tile(r, (1, nq))[:, :nq * n].reshape(nh, nq, n)
    bias = skew[:, :, nq - 1:nq - 1 + nk]
    if chunk_band:
        qc = jnp.arange(nq, dtype=jnp.int32)[:, None] // CHUNK
        kc = jnp.arange(nk, dtype=jnp.int32)[None, :] // CHUNK
        visible = (kc >= qc) & (kc <= qc + N_PREV_CHUNKS)
        bias = jnp.where(visible[None], bias, MASK_VALUE)
    return bias


def kernel(x_prompt, x_sample, cache_conv, state_ret, cache_att_k, cache_att_v, g_ffn1, w_ffn1_gu, w_ffn1_down, g_mix, g_ffn2, w_ffn2_gu, w_ffn2_down, w_conv_in, w_conv_dw, b_conv_dw, g_conv_norm, w_conv_out, w_ret_in, g_ret_out, w_ret_out, w_att_qkv, g_att_q, g_att_k, att_rel_bias, w_att_out):
    bp, tp, d = x_prompt.shape
    bs, ts, _ = x_sample.shape
    depth = g_mix.shape[0]
    n_p = bp * tp
    n_s = bs * ts
    assert bp == 1, "the prompt path carries conv / retention state along one sequence"

    ffn_stacks = ((w_ffn1_gu, w_ffn1_down), (w_ffn2_gu, w_ffn2_down))
    ffn_w = (w_ffn1_gu[0].astype(BF16), w_ffn1_down[0].astype(BF16))
    w_ret_out = w_ret_out.astype(BF16)

    pos = jnp.concatenate([jnp.arange(tp, dtype=jnp.int32),
                           jnp.tile(PAST_LEN + jnp.arange(ts, dtype=jnp.int32), bs)])
    half = RET_DK // 2
    freq = ROPE_BASE ** (-jnp.arange(half, dtype=F32) / half)
    ang = pos.astype(F32)[:, None] * freq[None, :]
    rope = (jnp.cos(ang), jnp.sin(ang))
    log_g = jnp.log1p(-jnp.exp2(-5.0 - jnp.arange(RET_HEADS, dtype=F32)))
    l_p = RET_L if tp % RET_L == 0 else tp
    l_s = ts

    conv_p, conv_s, ret_p, ret_s = [], [], [], []
    kp_l, vp_l, ks_l, vs_l = [], [], [], []
    xs = [x_prompt.reshape(n_p, d), x_sample.reshape(n_s, d)]
    for i in range(depth):
        kind, j = i % N_MIXERS, i // N_MIXERS
        x, *ffn_w = _ffn(xs, g_ffn1[i], *ffn_w, n_p, next_weights=ffn_stacks[1] + (i,))
        if kind == 0:
            u = _glu(x, g_mix[i], w_conv_in, j)
            z_p = _conv_prompt(u, n_p, w_conv_dw, b_conv_dw[j], g_conv_norm[j], j)
            z_s = _conv_sample(u, n_p, cache_conv, w_conv_dw, b_conv_dw[j], g_conv_norm[j], j)
            x = _out_proj(x, z_p, z_s, w_conv_out, j, PROJ_TN)
            conv_p.append(u[n_p - CONV_HALO:n_p].reshape(bp, CONV_HALO, d))
            conv_s.append(u[n_p:].reshape(bs, ts, d)[:, ts - CONV_HALO:])
        elif kind == 1:
            nq_t = RET_HEADS * RET_DK // PROJ_TN
            nv_t = RET_HEADS * RET_DV // PROJ_TN
            segments = [(nq_t, "rope_q", 0, False), (nq_t, "rope_k", 0, False),
                        (nv_t, "plain", 0, False), (nv_t, "silu", 0, False)]
            (qkvg,) = _proj(x, g_mix[i], w_ret_in, j, segments, rope=rope)
            zero_state = jnp.zeros((1, bp, RET_HEADS, RET_DK, RET_DV), F32)
            o_p, s_p = _retention(qkvg, zero_state, 0, g_ret_out[j], 0, bp, tp, l_p,
                                  (log_g, jnp.exp(log_g * l_p)))
            o_s, s_s = _retention(qkvg, state_ret, j, g_ret_out[j], n_p, bs, ts, l_s,
                                  (log_g, jnp.exp(log_g * l_s)))
            x = _out_proj(x, o_p, o_s, w_ret_out, j, PROJ_SUB)
            ret_p.append(s_p)
            ret_s.append(s_s)
        else:
            nd_t = d // PROJ_TN
            segments = [(nd_t, "headnorm", 0, False), (nd_t, "headnorm", 1, True), (nd_t, "plain", 0, True)]
            qkv, kv32 = _proj(x, g_mix[i], w_att_qkv, j, segments, gains=jnp.stack([g_att_q[j], g_att_k[j]]))
            k32, v32 = kv32[:, :d], kv32[:, d:]
            lc = cache_att_k.shape[2]
            o_p = _att_prompt(qkv, _band_bias(att_rel_bias[j], ATT_QB, ATT_KW, True), n_p)
            bias_s = _band_bias(att_rel_bias[j], ts, lc + ts, False)
            o_s = _att_sample(qkv, n_p, cache_att_k, cache_att_v, j, bias_s[:, :, :lc], bias_s[:, :, lc:])
            x = _out_proj(x, o_p, o_s, w_att_out, j, PROJ_TN)
            keep = min(BAND_PAST, tp)
            kp_l.append(k32[n_p - keep:n_p].reshape(bp, keep, ATT_HEADS, ATT_HD))
            vp_l.append(v32[n_p - keep:n_p].reshape(bp, keep, ATT_HEADS, ATT_HD))
            ks_l.append(k32[n_p:].reshape(bs, ts, ATT_HEADS, ATT_HD))
            vs_l.append(v32[n_p:].reshape(bs, ts, ATT_HEADS, ATT_HD))
        last = i == depth - 1
        xs = _ffn([x], g_ffn2[i], *ffn_w, n_p, split_out=last,
                  next_weights=None if last else ffn_stacks[0] + (i + 1,))
        if not last:
            xs, ffn_w = xs[:1], xs[1:]

    return (xs[0].reshape(bp, tp, d), xs[1].reshape(bs, ts, d),
            jnp.stack(conv_p), jnp.stack(conv_s),
            jnp.stack(ret_p), jnp.stack(ret_s),
            jnp.stack(kp_l), jnp.stack(vp_l),
            jnp.stack(ks_l), jnp.stack(vs_l))
```

```python
import functools

import jax
import jax.numpy as jnp
from jax import lax
from jax.experimental import pallas as pl
from jax.experimental.pallas import tpu as pltpu

F32 = jnp.float32
BF16 = jnp.bfloat16

CHUNK = 64
NORM_EPS = 1e-6
CONV_WIDTH = 31
CONV_HALO = CONV_WIDTH - 1
RET_HEADS = 8
RET_DK = 256
RET_DV = 512
ROPE_BASE = 10000.0
ATT_HEADS = 16
ATT_HD = 128
N_PREV_CHUNKS = 8
BAND_PAST = N_PREV_CHUNKS * CHUNK
REL_CLIP = 256
PAST_LEN = 4096
N_MIXERS = 3

LANES = 128
SUBLANES = 8
VMEM_BYTES = 64 * 1024 * 1024
VMEM_LIMIT_BYTES = 56 * 1024 * 1024
FFN_VMEM_LIMIT_BYTES = 60 * 1024 * 1024
assert FFN_VMEM_LIMIT_BYTES < VMEM_BYTES
MASK_VALUE = -1e30

FFN_TM = 1024
FFN_TM_SPLIT = 512
FFN_TF = 512
FFN_CAST_BLOCKS = 8
PROJ_TM = 1024
PROJ_TN = 1024
PROJ_SUB = 512
GLU_TN = 512
OUT_TM = 512
CONV_TT = 256
CONV_ROWS = 64
CONV_COLS = 128
CONV_PAD = 32
RET_L = 256
RET_HP = 2
ATT_QB = 256
ATT_KW = ATT_QB + BAND_PAST
ATT_HP = 4


def _params(*sem, vmem_limit_bytes=VMEM_LIMIT_BYTES):
    return pltpu.CompilerParams(dimension_semantics=sem, vmem_limit_bytes=vmem_limit_bytes)


def _as_bf16(w):
    return w if w.dtype == BF16 else w.astype(BF16)


def _rmsnorm_rows(x, g):
    ms = jnp.mean(x * x, axis=-1, keepdims=True)
    return x * lax.rsqrt(ms + NORM_EPS) * g


def _silu(x):
    return x * jax.nn.sigmoid(x)


def _row_tile_specs(tm, width, n_pt):
    whole = pl.BlockSpec((tm, width), lambda i, j: (i, 0))
    prompt = pl.BlockSpec((tm, width), lambda i, j: (jnp.minimum(i, n_pt - 1), 0))
    sample = pl.BlockSpec((tm, width), lambda i, j: (jnp.maximum(i - n_pt, 0), 0))
    return whole, prompt, sample


def _ffn_kernel(*refs, n_pt, split_in, split_out, cast_next):
    refs = list(refs)
    x_refs = [refs.pop(0) for _ in range(2 if split_in else 1)]
    g_ref, wg_ref, wu_ref, wd_ref = refs[:4]
    refs = refs[4:]
    h_ref = refs.pop()
    if cast_next:
        next_gu_ref, next_down_ref = refs[:2]
        cast_gu_ref, cast_down_ref = refs[-2:]
        refs = refs[2:-2]
    o_refs = refs
    i = pl.program_id(0)
    j = pl.program_id(1)

    def body(x_ref, o_ref):
        @pl.when(j == 0)
        def _():
            x = x_ref[...]
            h_ref[...] = _rmsnorm_rows(x, g_ref[...]).astype(BF16)
            o_ref[...] = x

        h = h_ref[...]
        gate = jnp.dot(h, wg_ref[...], preferred_element_type=F32)
        up = jnp.dot(h, wu_ref[...], preferred_element_type=F32)
        a = (_silu(gate) * (0.5 * up)).astype(BF16)
        o_ref[...] += jnp.dot(a, wd_ref[...], preferred_element_type=F32)

    if not (split_in or split_out):
        body(x_refs[0], o_refs[0])
    else:
        pl.when(i < n_pt)(lambda: body(x_refs[0], o_refs[0]))
        pl.when(i >= n_pt)(lambda: body(x_refs[-1], o_refs[-1]))

    if cast_next:
        @pl.when(i < FFN_CAST_BLOCKS)
        def _():
            cast_gu_ref[...] = next_gu_ref[...].astype(BF16)
            cast_down_ref[...] = next_down_ref[...].astype(BF16)


def _ffn(xs, g, w_gu, w_down, n_p, split_out=False, next_weights=None):
    split_in = len(xs) == 2
    d = xs[0].shape[1]
    t = sum(x.shape[0] for x in xs)
    f = w_down.shape[0]
    nf = f // FFN_TF
    tm = FFN_TM_SPLIT if split_out else FFN_TM
    whole, prompt, sample = _row_tile_specs(tm, d, n_p // tm)
    n_pt = n_p // tm
    in_prompt = pl.BlockSpec((tm, d), lambda i, j: (jnp.minimum(i, n_pt - 1), 0), pipeline_mode=pl.Buffered(1))
    in_sample = pl.BlockSpec((tm, d), lambda i, j: (jnp.maximum(i - n_pt, 0), 0), pipeline_mode=pl.Buffered(1))
    in_specs = ([in_prompt, in_sample] if split_in else [whole]) + [
        pl.BlockSpec((1, d), lambda i, j: (0, 0)),
        pl.BlockSpec((d, FFN_TF), lambda i, j: (0, j)),
        pl.BlockSpec((d, FFN_TF), lambda i, j: (0, j + nf)),
        pl.BlockSpec((FFN_TF, d), lambda i, j: (j, 0)),
    ]
    args = list(xs) + [g.reshape(1, d), w_gu, w_gu, w_down]
    out_specs = [prompt, sample] if split_out else [whole]
    out_shape = ([jax.ShapeDtypeStruct((n_p, d), F32), jax.ShapeDtypeStruct((t - n_p, d), F32)] if split_out
                 else [jax.ShapeDtypeStruct((t, d), F32)])
    if next_weights is not None:
        gu_stack, down_stack, layer = next_weights
        nb = FFN_CAST_BLOCKS
        assert t // tm >= nb
        bi = lambda i, j: jnp.where(i < nb, i, nb - 1)
        bj = lambda i, j: jnp.where(i < nb, j, nf - 1)
        gu_blk, down_blk = (d // nb, 2 * f // nf), (f // nf, d // nb)
        in_specs += [pl.BlockSpec((None,) + gu_blk, lambda i, j: (layer, bi(i, j), bj(i, j))),
                     pl.BlockSpec((None,) + down_blk, lambda i, j: (layer, bj(i, j), bi(i, j)))]
        args += [gu_stack, down_stack]
        out_specs += [pl.BlockSpec(gu_blk, lambda i, j: (bi(i, j), bj(i, j))),
                      pl.BlockSpec(down_blk, lambda i, j: (bj(i, j), bi(i, j)))]
        out_shape += [jax.ShapeDtypeStruct((d, 2 * f), BF16), jax.ShapeDtypeStruct((f, d), BF16)]
    kern = functools.partial(_ffn_kernel, n_pt=n_p // tm, split_in=split_in, split_out=split_out,
                             cast_next=next_weights is not None)
    return pl.pallas_call(
        kern,
        grid=(t // tm, nf),
        in_specs=in_specs,
        out_specs=out_specs,
        out_shape=out_shape,
        scratch_shapes=[pltpu.VMEM((tm, d), BF16)],
        compiler_params=_params("arbitrary", "arbitrary", vmem_limit_bytes=FFN_VMEM_LIMIT_BYTES),
        name="ffn",
    )(*args)


def _proj_kernel(*refs, segments, has_rope, has_gain, has_f32):
    it = iter(refs)
    x_ref, g_ref, w_ref = next(it), next(it), next(it)
    cos_ref = sin_ref = gh_ref = o32_ref = None
    if has_rope:
        cos_ref, sin_ref = next(it), next(it)
    if has_gain:
        gh_ref = next(it)
    o_ref = next(it)
    if has_f32:
        o32_ref = next(it)
    h_ref = next(it)
    j = pl.program_id(1)

    @pl.when(j == 0)
    def _():
        h_ref[...] = _rmsnorm_rows(x_ref[...], g_ref[...]).astype(BF16)

    def epilogue(y, mode, gain_row):
        if mode == "silu":
            return _silu(y)
        if mode in ("rope_q", "rope_k"):
            cos, sin = cos_ref[...], sin_ref[...]
            scale = RET_DK ** -0.5 if mode == "rope_k" else 1.0
            half = RET_DK // 2
            parts = []
            for hh in range(y.shape[1] // RET_DK):
                x1 = y[:, hh * RET_DK: hh * RET_DK + half]
                x2 = y[:, hh * RET_DK + half: (hh + 1) * RET_DK]
                parts.append((x1 * cos - x2 * sin) * scale)
                parts.append((x1 * sin + x2 * cos) * scale)
            return jnp.concatenate(parts, axis=1)
        if mode == "headnorm":
            gh = gh_ref[gain_row:gain_row + 1, :]
            return jnp.concatenate([_rmsnorm_rows(y[:, hh * ATT_HD:(hh + 1) * ATT_HD], gh)
                                    for hh in range(y.shape[1] // ATT_HD)], axis=1)
        return y

    def column_tile(mode, gain_row, f32_copy):
        for c0 in range(0, w_ref.shape[1], PROJ_SUB):
            cols = slice(c0, c0 + PROJ_SUB)
            y = jnp.dot(h_ref[...], _as_bf16(w_ref[:, cols]), preferred_element_type=F32)
            z = epilogue(y, mode, gain_row)
            o_ref[:, cols] = z.astype(o_ref.dtype)
            if f32_copy:
                o32_ref[:, cols] = z

    j0 = 0
    for n_tiles, mode, gain_row, f32_copy in segments:
        in_segment = jnp.logical_and(j >= j0, j < j0 + n_tiles)
        pl.when(in_segment)(functools.partial(column_tile, mode, gain_row, f32_copy))
        j0 += n_tiles


def _proj(x, g, w, layer, segments, *, rope=None, gains=None):
    t, d = x.shape
    tm, tn = PROJ_TM, PROJ_TN
    n_tiles = sum(seg[0] for seg in segments)
    n32_tiles = sum(seg[0] for seg in segments if seg[3])
    j32 = n_tiles - n32_tiles
    in_specs = [
        pl.BlockSpec((tm, d), lambda i, j: (i, 0)),
        pl.BlockSpec((1, d), lambda i, j: (0, 0)),
        pl.BlockSpec((None, d, tn), lambda i, j: (layer, 0, j)),
    ]
    args = [x, g.reshape(1, d), w]
    if rope is not None:
        in_specs += [pl.BlockSpec((tm, RET_DK // 2), lambda i, j: (i, 0))] * 2
        args += list(rope)
    if gains is not None:
        in_specs.append(pl.BlockSpec(gains.shape, lambda i, j: (0, 0)))
        args.append(gains)
    out_specs = [pl.BlockSpec((tm, tn), lambda i, j: (i, j))]
    out_shape = [jax.ShapeDtypeStruct((t, n_tiles * tn), BF16)]
    if n32_tiles:
        out_specs.append(pl.BlockSpec((tm, tn), lambda i, j: (i, jnp.maximum(j - j32, 0))))
        out_shape.append(jax.ShapeDtypeStruct((t, n32_tiles * tn), F32))
    kern = functools.partial(_proj_kernel, segments=tuple(segments), has_rope=rope is not None,
                             has_gain=gains is not None, has_f32=bool(n32_tiles))
    return pl.pallas_call(
        kern,
        grid=(t // tm, n_tiles),
        in_specs=in_specs,
        out_specs=out_specs,
        out_shape=out_shape,
        scratch_shapes=[pltpu.VMEM((tm, d), BF16)],
        compiler_params=_params("arbitrary", "arbitrary"),
        name="proj",
    )(*args)


def _glu_kernel(x_ref, g_ref, wa_ref, wb_ref, o_ref, h_ref):
    @pl.when(pl.program_id(1) == 0)
    def _():
        h_ref[...] = _rmsnorm_rows(x_ref[...], g_ref[...]).astype(BF16)

    h = h_ref[...]
    a = jnp.dot(h, _as_bf16(wa_ref[...]), preferred_element_type=F32)
    b = jnp.dot(h, _as_bf16(wb_ref[...]), preferred_element_type=F32)
    o_ref[...] = a * jax.nn.sigmoid(b)


def _glu(x, g, w_in, layer):
    t, d = x.shape
    tm, tn = PROJ_TM, GLU_TN
    nb = d // tn
    return pl.pallas_call(
        _glu_kernel,
        grid=(t // tm, nb),
        in_specs=[
            pl.BlockSpec((tm, d), lambda i, j: (i, 0)),
            pl.BlockSpec((1, d), lambda i, j: (0, 0)),
            pl.BlockSpec((None, d, tn), lambda i, j: (layer, 0, j)),
            pl.BlockSpec((None, d, tn), lambda i, j: (layer, 0, j + nb)),
        ],
        out_specs=pl.BlockSpec((tm, tn), lambda i, j: (i, j)),
        out_shape=jax.ShapeDtypeStruct((t, d), F32),
        scratch_shapes=[pltpu.VMEM((tm, d), BF16)],
        compiler_params=_params("parallel", "arbitrary"),
        name="conv_glu",
    )(x, g.reshape(1, d), w_in, w_in)


def _out_proj_kernel(x_ref, yp_ref, ys_ref, w_ref, o_ref, *scratch, n_pt):
    i = pl.program_id(0)
    if scratch:
        (wb_ref,) = scratch

        @pl.when(i == 0)
        def _():
            wb_ref[...] = w_ref[...].astype(BF16)
    else:
        wb_ref = w_ref

    def body(y_ref):
        for c0 in range(0, wb_ref.shape[1], PROJ_SUB):
            cols = slice(c0, c0 + PROJ_SUB)
            o_ref[:, cols] = x_ref[:, cols] + jnp.dot(y_ref[...], wb_ref[:, cols], preferred_element_type=F32)

    pl.when(i < n_pt)(lambda: body(yp_ref))
    pl.when(i >= n_pt)(lambda: body(ys_ref))


def _out_proj(x, y_p, y_s, w, layer):
    t, d = x.shape
    k = y_p.shape[1]
    tm = OUT_TM
    n_pt = y_p.shape[0] // tm
    row = lambda width: pl.BlockSpec((tm, width), lambda i: (i, 0))
    return pl.pallas_call(
        functools.partial(_out_proj_kernel, n_pt=n_pt),
        grid=(t // tm,),
        in_specs=[
            row(d),
            pl.BlockSpec((tm, k), lambda i: (jnp.minimum(i, n_pt - 1), 0)),
            pl.BlockSpec((tm, k), lambda i: (jnp.maximum(i - n_pt, 0), 0)),
            pl.BlockSpec((None, k, d), lambda i: (layer, 0, 0), pipeline_mode=pl.Buffered(1)),
        ],
        out_specs=row(d),
        out_shape=jax.ShapeDtypeStruct((t, d), F32),
        scratch_shapes=[] if w.dtype == BF16 else [pltpu.VMEM((k, d), BF16)],
        compiler_params=_params("arbitrary"),
        name="out_proj",
    )(x, y_p, y_s, w)


def _conv_kernel(*refs, tt, carry):
    if carry:
        u_ref, w_ref, b_ref, gn_ref, z_ref, ext_ref, y_ref = refs
        halo_ref = None
    else:
        u_ref, halo_ref, w_ref, b_ref, gn_ref, z_ref, ext_ref, y_ref = refs
    d = u_ref.shape[-1]
    off = CONV_PAD - CONV_HALO
    rows = min(CONV_ROWS, tt)
    slab = rows + CONV_PAD

    if carry:
        @pl.when(pl.program_id(0) == 0)
        def _():
            ext_ref[0:CONV_PAD, :] = jnp.zeros((CONV_PAD, d), F32)
    else:
        ext_ref[0:SUBLANES, :] = jnp.zeros((SUBLANES, d), F32)
        ext_ref[off:CONV_PAD, :] = halo_ref[...]
    ext_ref[CONV_PAD:CONV_PAD + tt, :] = u_ref[...]

    def row_chunk(r, _):
        r0 = pl.multiple_of(r * rows, rows)
        for c0 in range(0, d, CONV_COLS):
            cols = slice(c0, c0 + CONV_COLS)
            x = ext_ref[pl.ds(r0, slab), cols]
            acc = jnp.zeros((rows, CONV_COLS), F32)
            for s in range(SUBLANES):
                xs = x if s == 0 else pltpu.roll(x, slab - s, 0)
                for a8 in range(0, CONV_PAD + SUBLANES, SUBLANES):
                    w = a8 + s - off
                    if 0 <= w < CONV_WIDTH:
                        acc = acc + xs[a8:a8 + rows, :] * w_ref[w:w + 1, cols]
            y_ref[pl.ds(r0, rows), cols] = acc + b_ref[:, cols]
        return 0

    lax.fori_loop(0, tt // rows, row_chunk, 0)

    z_ref[...] = _silu(_rmsnorm_rows(y_ref[...], gn_ref[...])).astype(z_ref.dtype)

    if carry:
        ext_ref[0:CONV_PAD, :] = ext_ref[tt:tt + CONV_PAD, :]


def _conv_prompt(u, n_rows, w_dw, b_dw, g_norm, layer):
    d = u.shape[1]
    tt = CONV_TT
    kern = functools.partial(_conv_kernel, tt=tt, carry=True)
    return pl.pallas_call(
        kern,
        grid=(n_rows // tt,),
        in_specs=[
            pl.BlockSpec((tt, d), lambda i: (i, 0)),
            pl.BlockSpec((None, CONV_WIDTH, d), lambda i: (layer, 0, 0)),
            pl.BlockSpec((1, d), lambda i: (0, 0)),
            pl.BlockSpec((1, d), lambda i: (0, 0)),
        ],
        out_specs=pl.BlockSpec((tt, d), lambda i: (i, 0)),
        out_shape=jax.ShapeDtypeStruct((n_rows, d), BF16),
        scratch_shapes=[pltpu.VMEM((CONV_PAD + tt, d), F32), pltpu.VMEM((tt, d), F32)],
        compiler_params=_params("arbitrary"),
        name="conv_prompt",
    )(u, w_dw, b_dw.reshape(1, d), g_norm.reshape(1, d))


def _conv_sample(u, row0, cache, w_dw, b_dw, g_norm, layer):
    _, nb, _, d = cache.shape
    tt = (u.shape[0] - row0) // nb
    rb0 = row0 // tt
    kern = functools.partial(_conv_kernel, tt=tt, carry=False)
    return pl.pallas_call(
        kern,
        grid=(nb,),
        in_specs=[
            pl.BlockSpec((tt, d), lambda b: (b + rb0, 0)),
            pl.BlockSpec((None, None, CONV_HALO, d), lambda b: (layer, b, 0, 0)),
            pl.BlockSpec((None, CONV_WIDTH, d), lambda b: (layer, 0, 0)),
            pl.BlockSpec((1, d), lambda b: (0, 0)),
            pl.BlockSpec((1, d), lambda b: (0, 0)),
        ],
        out_specs=pl.BlockSpec((tt, d), lambda b: (b, 0)),
        out_shape=jax.ShapeDtypeStruct((nb * tt, d), BF16),
        scratch_shapes=[pltpu.VMEM((CONV_PAD + tt, d), F32), pltpu.VMEM((tt, d), F32)],
        compiler_params=_params("parallel"),
        name="conv_sample",
    )(u, cache, w_dw, b_dw.reshape(1, d), g_norm.reshape(1, d))


def _ret_kernel(lg_ref, gl_ref, q_ref, k_ref, v_ref, sg_ref, s0_ref, go_ref, o_ref, sout_ref, s_ref, *, L):
    hg = pl.program_id(1)
    c = pl.program_id(2)

    @pl.when(c == 0)
    def _():
        s_ref[...] = s0_ref[...]

    ri = lax.broadcasted_iota(jnp.int32, (L, L), 0)
    ci = lax.broadcasted_iota(jnp.int32, (L, L), 1)
    diff = (ri - ci).astype(F32)
    row = lax.broadcasted_iota(jnp.int32, (L, 1), 0).astype(F32)
    for hh in range(RET_HP):
        lg = lg_ref[hg * RET_HP + hh]
        g_chunk = gl_ref[hg * RET_HP + hh]
        kcols = slice(hh * RET_DK, (hh + 1) * RET_DK)
        vcols = slice(hh * RET_DV, (hh + 1) * RET_DV)
        q, k, v = q_ref[:, kcols], k_ref[:, kcols], v_ref[:, vcols]
        decay = jnp.where(diff >= 0, jnp.exp(lg * jnp.maximum(diff, 0.0)), 0.0)
        inner = lax.dot_general(q, k, (((1,), (1,)), ((), ())), preferred_element_type=F32) * decay
        o = jnp.dot(inner.astype(BF16), v, preferred_element_type=F32)

        state = s_ref[hh]
        q_dec = jnp.exp(lg * (row + 1.0))
        o = o + jnp.dot(q, state.astype(BF16), preferred_element_type=F32) * q_dec
        k_dec = jnp.exp(lg * (L - 1.0 - row))
        kd = (k.astype(F32) * k_dec).astype(BF16)
        new_state = state * g_chunk + lax.dot_general(kd, v, (((0,), (0,)), ((), ())),
                                                      preferred_element_type=F32)
        s_ref[hh] = new_state

        o = _rmsnorm_rows(o, go_ref[:, vcols])
        o_ref[:, vcols] = (sg_ref[:, vcols].astype(F32) * o).astype(o_ref.dtype)

    @pl.when(c == pl.num_programs(2) - 1)
    def _():
        sout_ref[...] = s_ref[...]


def _retention(qkvg, s0, layer, g_out, row0, n_seq, seq_len, L, decay_consts):
    nc = seq_len // L
    rb0 = row0 // L
    hp, dk, dv = RET_HP, RET_DK, RET_DV
    ng = RET_HEADS // hp
    row_map = lambda b, g, c: b * nc + c + rb0
    kern = functools.partial(_ret_kernel, L=L)
    lg, gl = decay_consts
    v0 = 2 * ng * dk // dv
    o, s_out = pl.pallas_call(
        kern,
        grid=(n_seq, ng, nc),
        in_specs=[
            pl.BlockSpec(memory_space=pltpu.SMEM),
            pl.BlockSpec(memory_space=pltpu.SMEM),
            pl.BlockSpec((L, hp * dk), lambda b, g, c: (row_map(b, g, c), g)),
            pl.BlockSpec((L, hp * dk), lambda b, g, c: (row_map(b, g, c), g + ng)),
            pl.BlockSpec((L, hp * dv), lambda b, g, c: (row_map(b, g, c), g + v0)),
            pl.BlockSpec((L, hp * dv), lambda b, g, c: (row_map(b, g, c), g + v0 + ng)),
            pl.BlockSpec((None, None, hp, dk, dv), lambda b, g, c: (layer, b, g, 0, 0)),
            pl.BlockSpec((1, hp * dv), lambda b, g, c: (0, g)),
        ],
        out_specs=[
            pl.BlockSpec((L, hp * dv), lambda b, g, c: (b * nc + c, g)),
            pl.BlockSpec((None, hp, dk, dv), lambda b, g, c: (b, g, 0, 0)),
        ],
        out_shape=[
            jax.ShapeDtypeStruct((n_seq * seq_len, RET_HEADS * dv), BF16),
            jax.ShapeDtypeStruct((n_seq, RET_HEADS, dk, dv), F32),
        ],
        scratch_shapes=[pltpu.VMEM((hp, dk, dv), F32)],
        compiler_params=_params("parallel", "parallel", "arbitrary"),
        name="retention",
    )(lg, gl, qkvg, qkvg, qkvg, qkvg, s0, g_out.reshape(1, RET_HEADS * dv))
    return o, s_out


def _softmax_weights(s):
    m = jnp.max(s, axis=-1, keepdims=True)
    p = jnp.exp(s - m)
    return p, jnp.sum(p, axis=-1, keepdims=True)


def _att_prompt_kernel(q_ref, k0_ref, k1_ref, k2_ref, v0_ref, v1_ref, v2_ref, bias_ref, o_ref):
    qb = pl.program_id(1)
    k_pos = qb * ATT_QB - BAND_PAST + lax.broadcasted_iota(jnp.int32, (1, ATT_KW), 1)
    pos_mask = jnp.where(k_pos >= 0, 0.0, MASK_VALUE).astype(F32)
    for hh in range(ATT_HP):
        cols = slice(hh * ATT_HD, (hh + 1) * ATT_HD)
        q = q_ref[:, cols]
        k = jnp.concatenate([k0_ref[:, cols], k1_ref[:, cols], k2_ref[:, cols]], axis=0)
        v = jnp.concatenate([v0_ref[:, cols], v1_ref[:, cols], v2_ref[:, cols]], axis=0)
        s = lax.dot_general(q, k, (((1,), (1,)), ((), ())), preferred_element_type=F32)
        s = s * (ATT_HD ** -0.5) + bias_ref[hh] + pos_mask
        p, l = _softmax_weights(s)
        o = jnp.dot(p.astype(BF16), v, preferred_element_type=F32)
        o_ref[:, cols] = (o / l).astype(o_ref.dtype)


def _att_prompt(qkv, bias, n_rows):
    nqb = n_rows // ATT_QB
    nw = ATT_KW // ATT_QB
    ng = ATT_HEADS // ATT_HP
    blk = (ATT_QB, ATT_HP * ATT_HD)

    def kv_spec(back, section):
        return pl.BlockSpec(blk, lambda g, i: (jnp.maximum(i - back, 0), g + section * ng))

    k_specs = [kv_spec(nw - 1 - n, 1) for n in range(nw)]
    v_specs = [kv_spec(nw - 1 - n, 2) for n in range(nw)]
    return pl.pallas_call(
        _att_prompt_kernel,
        grid=(ng, nqb),
        in_specs=[pl.BlockSpec(blk, lambda g, i: (i, g))] + k_specs + v_specs + [
            pl.BlockSpec((ATT_HP, ATT_QB, ATT_KW), lambda g, i: (g, 0, 0)),
        ],
        out_specs=pl.BlockSpec(blk, lambda g, i: (i, g)),
        out_shape=jax.ShapeDtypeStruct((n_rows, ATT_HEADS * ATT_HD), BF16),
        compiler_params=_params("parallel", "arbitrary"),
        name="att_prompt",
    )(*([qkv] * (1 + 2 * nw)), bias)


def _att_sample_kernel(q_ref, kn_ref, vn_ref, kc_ref, vc_ref, bc_ref, bn_ref, o_ref, kd_ref, vd_ref):
    scale = ATT_HD ** -0.5
    dims = (((1,), (1,)), ((), ()))
    lc = kc_ref.shape[0] // ATT_HEADS
    kd_ref[...] = pltpu.einshape("(th)d->htd", kc_ref[...], h=ATT_HEADS).astype(BF16)
    vd_ref[...] = pltpu.einshape("(th)d->htd", vc_ref[...], h=ATT_HEADS).astype(BF16)
    for h in range(ATT_HEADS):
        cols = slice(h * ATT_HD, (h + 1) * ATT_HD)
        q = q_ref[:, cols]
        kc = kd_ref[h]
        vc = vd_ref[h]
        sc = lax.dot_general(q, kc, dims, preferred_element_type=F32) * scale + bc_ref[h]
        sn = lax.dot_general(q, kn_ref[:, cols], dims, preferred_element_type=F32) * scale + bn_ref[h]
        m = jnp.maximum(jnp.max(sc, axis=-1, keepdims=True), jnp.max(sn, axis=-1, keepdims=True))
        pc = jnp.exp(sc - m)
        pn = jnp.exp(sn - m)
        l = jnp.sum(pc, axis=-1, keepdims=True) + jnp.sum(pn, axis=-1, keepdims=True)
        o = jnp.dot(pc.astype(BF16), vc, preferred_element_type=F32)
        o = o + jnp.dot(pn.astype(BF16), vn_ref[:, cols], preferred_element_type=F32)
        o_ref[:, cols] = (o / l).astype(o_ref.dtype)


def _att_sample(qkv, row0, cache_k, cache_v, layer, bias_c, bias_n):
    _, nb, lc, nh, hd = cache_k.shape
    d = nh * hd
    ts = (qkv.shape[0] - row0) // nb
    rb0 = row0 // ts
    new_specs = [pl.BlockSpec((ts, d), lambda b, section=section: (b + rb0, section)) for section in range(3)]
    cache_k = cache_k.reshape(cache_k.shape[0], nb, lc * nh, hd)
    cache_v = cache_v.reshape(cache_v.shape[0], nb, lc * nh, hd)
    cache_spec = pl.BlockSpec((None, None, lc * nh, hd), lambda b: (layer, b, 0, 0))
    return pl.pallas_call(
        _att_sample_kernel,
        grid=(nb,),
        in_specs=new_specs + [cache_spec, cache_spec,
                              pl.BlockSpec((nh, ts, lc), lambda b: (0, 0, 0)),
                              pl.BlockSpec((nh, ts, ts), lambda b: (0, 0, 0))],
        out_specs=pl.BlockSpec((ts, d), lambda b: (b, 0)),
        out_shape=jax.ShapeDtypeStruct((nb * ts, d), BF16),
        scratch_shapes=[pltpu.VMEM((nh, lc, hd), BF16), pltpu.VMEM((nh, lc, hd), BF16)],
        compiler_params=_params("parallel"),
        name="att_sample",
    )(qkv, qkv, qkv, cache_k, cache_v, bias_c, bias_n)


def _band_bias(rel_table, nq, nk, chunk_band):
    nh = rel_table.shape[0]
    n = nq + nk - 1
    q_minus_k = nq - 1 - jnp.arange(n, dtype=jnp.int32) + BAND_PAST
    r = rel_table[:, jnp.clip(q_minus_k, -REL_CLIP, REL_CLIP) + REL_CLIP].astype(F32)
    r = jnp.pad(r, ((0, 0), (0, 1)))
    skew = jnp.tile(r, (1, nq))[:, :nq * n].reshape(nh, nq, n)
    bias = skew[:, :, nq - 1:nq - 1 + nk]
    if chunk_band:
        qc = jnp.arange(nq, dtype=jnp.int32)[:, None] // CHUNK
        kc = jnp.arange(nk, dtype=jnp.int32)[None, :] // CHUNK
        visible = (kc >= qc) & (kc <= qc + N_PREV_CHUNKS)
        bias = jnp.where(visible[None], bias, MASK_VALUE)
    return bias


def kernel(x_prompt, x_sample, cache_conv, state_ret, cache_att_k, cache_att_v, g_ffn1, w_ffn1_gu, w_ffn1_down, g_mix, g_ffn2, w_ffn2_gu, w_ffn2_down, w_conv_in, w_conv_dw, b_conv_dw, g_conv_norm, w_conv_out, w_ret_in, g_ret_out, w_ret_out, w_att_qkv, g_att_q, g_att_k, att_rel_bias, w_att_out):
    bp, tp, d = x_prompt.shape
    bs, ts, _ = x_sample.shape
    depth = g_mix.shape[0]
    n_p = bp * tp
    n_s = bs * ts
    assert bp == 1, "the prompt path carries conv / retention state along one sequence"

    ffn_stacks = ((w_ffn1_gu, w_ffn1_down), (w_ffn2_gu, w_ffn2_down))
    ffn_w = (w_ffn1_gu[0].astype(BF16), w_ffn1_down[0].astype(BF16))
    w_ret_out = w_ret_out.astype(BF16)

    pos = jnp.concatenate([jnp.arange(tp, dtype=jnp.int32),
                           jnp.tile(PAST_LEN + jnp.arange(ts, dtype=jnp.int32), bs)])
    half = RET_DK // 2
    freq = ROPE_BASE ** (-jnp.arange(half, dtype=F32) / half)
    ang = pos.astype(F32)[:, None] * freq[None, :]
    rope = (jnp.cos(ang), jnp.sin(ang))
    log_g = jnp.log1p(-jnp.exp2(-5.0 - jnp.arange(RET_HEADS, dtype=F32)))
    l_p = RET_L if tp % RET_L == 0 else tp
    l_s = ts

    conv_p, conv_s, ret_p, ret_s = [], [], [], []
    kp_l, vp_l, ks_l, vs_l = [], [], [], []
    xs = [x_prompt.reshape(n_p, d), x_sample.reshape(n_s, d)]
    for i in range(depth):
        kind, j = i % N_MIXERS, i // N_MIXERS
        x, *ffn_w = _ffn(xs, g_ffn1[i], *ffn_w, n_p, next_weights=ffn_stacks[1] + (i,))
        if kind == 0:
            u = _glu(x, g_mix[i], w_conv_in, j)
            z_p = _conv_prompt(u, n_p, w_conv_dw, b_conv_dw[j], g_conv_norm[j], j)
            z_s = _conv_sample(u, n_p, cache_conv, w_conv_dw, b_conv_dw[j], g_conv_norm[j], j)
            x = _out_proj(x, z_p, z_s, w_conv_out, j)
            conv_p.append(u[n_p - CONV_HALO:n_p].reshape(bp, CONV_HALO, d))
            conv_s.append(u[n_p:].reshape(bs, ts, d)[:, ts - CONV_HALO:])
        elif kind == 1:
            nq_t = RET_HEADS * RET_DK // PROJ_TN
            nv_t = RET_HEADS * RET_DV // PROJ_TN
            segments = [(nq_t, "rope_q", 0, False), (nq_t, "rope_k", 0, False),
                        (nv_t, "plain", 0, False), (nv_t, "silu", 0, False)]
            (qkvg,) = _proj(x, g_mix[i], w_ret_in, j, segments, rope=rope)
            zero_state = jnp.zeros((1, bp, RET_HEADS, RET_DK, RET_DV), F32)
            o_p, s_p = _retention(qkvg, zero_state, 0, g_ret_out[j], 0, bp, tp, l_p,
                                  (log_g, jnp.exp(log_g * l_p)))
            o_s, s_s = _retention(qkvg, state_ret, j, g_ret_out[j], n_p, bs, ts, l_s,
                                  (log_g, jnp.exp(log_g * l_s)))
            x = _out_proj(x, o_p, o_s, w_ret_out, j)
            ret_p.append(s_p)
            ret_s.append(s_s)
        else:
            nd_t = d // PROJ_TN
            segments = [(nd_t, "headnorm", 0, False), (nd_t, "headnorm", 1, True), (nd_t, "plain", 0, True)]
            qkv, kv32 = _proj(x, g_mix[i], w_att_qkv, j, segments, gains=jnp.stack([g_att_q[j], g_att_k[j]]))
            k32, v32 = kv32[:, :d], kv32[:, d:]
            lc = cache_att_k.shape[2]
            o_p = _att_prompt(qkv, _band_bias(att_rel_bias[j], ATT_QB, ATT_KW, True), n_p)
            bias_s = _band_bias(att_rel_bias[j], ts, lc + ts, False)
            o_s = _att_sample(qkv, n_p, cache_att_k, cache_att_v, j, bias_s[:, :, :lc], bias_s[:, :, lc:])
            x = _out_proj(x, o_p, o_s, w_att_out, j)
            keep = min(BAND_PAST, tp)
            kp_l.append(k32[n_p - keep:n_p].reshape(bp, keep, ATT_HEADS, ATT_HD))
            vp_l.append(v32[n_p - keep:n_p].reshape(bp, keep, ATT_HEADS, ATT_HD))
            ks_l.append(k32[n_p:].reshape(bs, ts, ATT_HEADS, ATT_HD))
            vs_l.append(v32[n_p:].reshape(bs, ts, ATT_HEADS, ATT_HD))
        last = i == depth - 1
        xs = _ffn([x], g_ffn2[i], *ffn_w, n_p, split_out=last,
                  next_weights=None if last else ffn_stacks[0] + (i + 1,))
        if not last:
            xs, ffn_w = xs[:1], xs[1:]

    return (xs[0].reshape(bp, tp, d), xs[1].reshape(bs, ts, d),
            jnp.stack(conv_p), jnp.stack(conv_s),
            jnp.stack(ret_p), jnp.stack(ret_s),
            jnp.stack(kp_l), jnp.stack(vp_l),
            jnp.stack(ks_l), jnp.stack(vs_l))
```

```python
import functools

import jax
import jax.numpy as jnp
from jax import lax
from jax.experimental import pallas as pl
from jax.experimental.pallas import tpu as pltpu

F32 = jnp.float32
BF16 = jnp.bfloat16

CHUNK = 64
NORM_EPS = 1e-6
CONV_WIDTH = 31
CONV_HALO = CONV_WIDTH - 1
RET_HEADS = 8
RET_DK = 256
RET_DV = 512
ROPE_BASE = 10000.0
ATT_HEADS = 16
ATT_HD = 128
N_PREV_CHUNKS = 8
BAND_PAST = N_PREV_CHUNKS * CHUNK
REL_CLIP = 256
PAST_LEN = 4096
N_MIXERS = 3

LANES = 128
SUBLANES = 8
VMEM_BYTES = 64 * 1024 * 1024
VMEM_LIMIT_BYTES = 56 * 1024 * 1024
FFN_VMEM_LIMIT_BYTES = 60 * 1024 * 1024
assert FFN_VMEM_LIMIT_BYTES < VMEM_BYTES
MASK_VALUE = -1e30

FFN_TM = 1024
FFN_TM_SPLIT = 512
FFN_TF = 512
FFN_CAST_BLOCKS = 8
PROJ_TM = 1024
PROJ_TN = 1024
PROJ_SUB = 512
GLU_TN = 512
OUT_TM = 512
CONV_TT = 256
CONV_ROWS = 64
CONV_COLS = 128
CONV_PAD = 32
RET_L = 256
RET_HP = 2
ATT_QB = 256
ATT_KW = ATT_QB + BAND_PAST
ATT_HP = 4


def _params(*sem, vmem_limit_bytes=VMEM_LIMIT_BYTES):
    return pltpu.CompilerParams(dimension_semantics=sem, vmem_limit_bytes=vmem_limit_bytes)


def _as_bf16(w):
    return w if w.dtype == BF16 else w.astype(BF16)


def _rmsnorm_rows(x, g):
    ms = jnp.mean(x * x, axis=-1, keepdims=True)
    return x * lax.rsqrt(ms + NORM_EPS) * g


def _silu(x):
    return x * jax.nn.sigmoid(x)


def _row_tile_specs(tm, width, n_pt):
    whole = pl.BlockSpec((tm, width), lambda i, j: (i, 0))
    prompt = pl.BlockSpec((tm, width), lambda i, j: (jnp.minimum(i, n_pt - 1), 0))
    sample = pl.BlockSpec((tm, width), lambda i, j: (jnp.maximum(i - n_pt, 0), 0))
    return whole, prompt, sample


def _ffn_kernel(*refs, n_pt, split_in, split_out, cast_next):
    refs = list(refs)
    x_refs = [refs.pop(0) for _ in range(2 if split_in else 1)]
    g_ref, wg_ref, wu_ref, wd_ref = refs[:4]
    refs = refs[4:]
    h_ref = refs.pop()
    if cast_next:
        next_gu_ref, next_down_ref = refs[:2]
        cast_gu_ref, cast_down_ref = refs[-2:]
        refs = refs[2:-2]
    o_refs = refs
    i = pl.program_id(0)
    j = pl.program_id(1)

    def body(x_ref, o_ref):
        @pl.when(j == 0)
        def _():
            x = x_ref[...]
            h_ref[...] = _rmsnorm_rows(x, g_ref[...]).astype(BF16)
            o_ref[...] = x

        h = h_ref[...]
        gate = jnp.dot(h, wg_ref[...], preferred_element_type=F32)
        up = jnp.dot(h, wu_ref[...], preferred_element_type=F32)
        a = (_silu(gate) * (0.5 * up)).astype(BF16)
        o_ref[...] += jnp.dot(a, wd_ref[...], preferred_element_type=F32)

    if not (split_in or split_out):
        body(x_refs[0], o_refs[0])
    else:
        pl.when(i < n_pt)(lambda: body(x_refs[0], o_refs[0]))
        pl.when(i >= n_pt)(lambda: body(x_refs[-1], o_refs[-1]))

    if cast_next:
        @pl.when(i < FFN_CAST_BLOCKS)
        def _():
            cast_gu_ref[...] = next_gu_ref[...].astype(BF16)
            cast_down_ref[...] = next_down_ref[...].astype(BF16)


def _ffn(xs, g, w_gu, w_down, n_p, split_out=False, next_weights=None):
    split_in = len(xs) == 2
    d = xs[0].shape[1]
    t = sum(x.shape[0] for x in xs)
    f = w_down.shape[0]
    nf = f // FFN_TF
    tm = FFN_TM_SPLIT if split_out else FFN_TM
    whole, prompt, sample = _row_tile_specs(tm, d, n_p // tm)
    n_pt = n_p // tm
    in_prompt = pl.BlockSpec((tm, d), lambda i, j: (jnp.minimum(i, n_pt - 1), 0), pipeline_mode=pl.Buffered(1))
    in_sample = pl.BlockSpec((tm, d), lambda i, j: (jnp.maximum(i - n_pt, 0), 0), pipeline_mode=pl.Buffered(1))
    in_specs = ([in_prompt, in_sample] if split_in else [whole]) + [
        pl.BlockSpec((1, d), lambda i, j: (0, 0)),
        pl.BlockSpec((d, FFN_TF), lambda i, j: (0, j)),
        pl.BlockSpec((d, FFN_TF), lambda i, j: (0, j + nf)),
        pl.BlockSpec((FFN_TF, d), lambda i, j: (j, 0)),
    ]
    args = list(xs) + [g.reshape(1, d), w_gu, w_gu, w_down]
    out_specs = [prompt, sample] if split_out else [whole]
    out_shape = ([jax.ShapeDtypeStruct((n_p, d), F32), jax.ShapeDtypeStruct((t - n_p, d), F32)] if split_out
                 else [jax.ShapeDtypeStruct((t, d), F32)])
    if next_weights is not None:
        gu_stack, down_stack, layer = next_weights
        nb = FFN_CAST_BLOCKS
        assert t // tm >= nb
        bi = lambda i, j: jnp.where(i < nb, i, nb - 1)
        bj = lambda i, j: jnp.where(i < nb, j, nf - 1)
        gu_blk, down_blk = (d // nb, 2 * f // nf), (f // nf, d // nb)
        in_specs += [pl.BlockSpec((None,) + gu_blk, lambda i, j: (layer, bi(i, j), bj(i, j))),
                     pl.BlockSpec((None,) + down_blk, lambda i, j: (layer, bj(i, j), bi(i, j)))]
        args += [gu_stack, down_stack]
        out_specs += [pl.BlockSpec(gu_blk, lambda i, j: (bi(i, j), bj(i, j))),
                      pl.BlockSpec(down_blk, lambda i, j: (bj(i, j), bi(i, j)))]
        out_shape += [jax.ShapeDtypeStruct((d, 2 * f), BF16), jax.ShapeDtypeStruct((f, d), BF16)]
    kern = functools.partial(_ffn_kernel, n_pt=n_p // tm, split_in=split_in, split_out=split_out,
                             cast_next=next_weights is not None)
    return pl.pallas_call(
        kern,
        grid=(t // tm, nf),
        in_specs=in_specs,
        out_specs=out_specs,
        out_shape=out_shape,
        scratch_shapes=[pltpu.VMEM((tm, d), BF16)],
        compiler_params=_params("arbitrary", "arbitrary", vmem_limit_bytes=FFN_VMEM_LIMIT_BYTES),
        name="ffn",
    )(*args)


def _proj_kernel(*refs, segments, has_rope, has_gain, has_f32):
    it = iter(refs)
    x_ref, g_ref, w_ref = next(it), next(it), next(it)
    cos_ref = sin_ref = gh_ref = o32_ref = None
    if has_rope:
        cos_ref, sin_ref = next(it), next(it)
    if has_gain:
        gh_ref = next(it)
    o_ref = next(it)
    if has_f32:
        o32_ref = next(it)
    h_ref = next(it)
    j = pl.program_id(1)

    @pl.when(j == 0)
    def _():
        h_ref[...] = _rmsnorm_rows(x_ref[...], g_ref[...]).astype(BF16)

    def epilogue(y, mode, gain_row):
        if mode == "silu":
            return _silu(y)
        if mode in ("rope_q", "rope_k"):
            cos, sin = cos_ref[...], sin_ref[...]
            scale = RET_DK ** -0.5 if mode == "rope_k" else 1.0
            half = RET_DK // 2
            parts = []
            for hh in range(y.shape[1] // RET_DK):
                x1 = y[:, hh * RET_DK: hh * RET_DK + half]
                x2 = y[:, hh * RET_DK + half: (hh + 1) * RET_DK]
                parts.append((x1 * cos - x2 * sin) * scale)
                parts.append((x1 * sin + x2 * cos) * scale)
            return jnp.concatenate(parts, axis=1)
        if mode == "headnorm":
            gh = gh_ref[gain_row:gain_row + 1, :]
            return jnp.concatenate([_rmsnorm_rows(y[:, hh * ATT_HD:(hh + 1) * ATT_HD], gh)
                                    for hh in range(y.shape[1] // ATT_HD)], axis=1)
        return y

    def column_tile(mode, gain_row, f32_copy):
        for c0 in range(0, w_ref.shape[1], PROJ_SUB):
            cols = slice(c0, c0 + PROJ_SUB)
            y = jnp.dot(h_ref[...], _as_bf16(w_ref[:, cols]), preferred_element_type=F32)
            z = epilogue(y, mode, gain_row)
            o_ref[:, cols] = z.astype(o_ref.dtype)
            if f32_copy:
                o32_ref[:, cols] = z

    j0 = 0
    for n_tiles, mode, gain_row, f32_copy in segments:
        in_segment = jnp.logical_and(j >= j0, j < j0 + n_tiles)
        pl.when(in_segment)(functools.partial(column_tile, mode, gain_row, f32_copy))
        j0 += n_tiles


def _proj(x, g, w, layer, segments, *, rope=None, gains=None):
    t, d = x.shape
    tm, tn = PROJ_TM, PROJ_TN
    n_tiles = sum(seg[0] for seg in segments)
    n32_tiles = sum(seg[0] for seg in segments if seg[3])
    j32 = n_tiles - n32_tiles
    in_specs = [
        pl.BlockSpec((tm, d), lambda i, j: (i, 0)),
        pl.BlockSpec((1, d), lambda i, j: (0, 0)),
        pl.BlockSpec((None, d, tn), lambda i, j: (layer, 0, j)),
    ]
    args = [x, g.reshape(1, d), w]
    if rope is not None:
        in_specs += [pl.BlockSpec((tm, RET_DK // 2), lambda i, j: (i, 0))] * 2
        args += list(rope)
    if gains is not None:
        in_specs.append(pl.BlockSpec(gains.shape, lambda i, j: (0, 0)))
        args.append(gains)
    out_specs = [pl.BlockSpec((tm, tn), lambda i, j: (i, j))]
    out_shape = [jax.ShapeDtypeStruct((t, n_tiles * tn), BF16)]
    if n32_tiles:
        out_specs.append(pl.BlockSpec((tm, tn), lambda i, j: (i, jnp.maximum(j - j32, 0))))
        out_shape.append(jax.ShapeDtypeStruct((t, n32_tiles * tn), F32))
    kern = functools.partial(_proj_kernel, segments=tuple(segments), has_rope=rope is not None,
                             has_gain=gains is not None, has_f32=bool(n32_tiles))
    return pl.pallas_call(
        kern,
        grid=(t // tm, n_tiles),
        in_specs=in_specs,
        out_specs=out_specs,
        out_shape=out_shape,
        scratch_shapes=[pltpu.VMEM((tm, d), BF16)],
        compiler_params=_params("arbitrary", "arbitrary"),
        name="proj",
    )(*args)


def _glu_kernel(x_ref, g_ref, wa_ref, wb_ref, o_ref, h_ref):
    @pl.when(pl.program_id(1) == 0)
    def _():
        h_ref[...] = _rmsnorm_rows(x_ref[...], g_ref[...]).astype(BF16)

    h = h_ref[...]
    a = jnp.dot(h, _as_bf16(wa_ref[...]), preferred_element_type=F32)
    b = jnp.dot(h, _as_bf16(wb_ref[...]), preferred_element_type=F32)
    o_ref[...] = a * jax.nn.sigmoid(b)


def _glu(x, g, w_in, layer, row0):
    d = x.shape[1]
    t = x.shape[0] - row0
    tm, tn = PROJ_TM, GLU_TN
    nb = d // tn
    rb0 = row0 // tm
    return pl.pallas_call(
        _glu_kernel,
        grid=(t // tm, nb),
        in_specs=[
            pl.BlockSpec((tm, d), lambda i, j: (i + rb0, 0)),
            pl.BlockSpec((1, d), lambda i, j: (0, 0)),
            pl.BlockSpec((None, d, tn), lambda i, j: (layer, 0, j)),
            pl.BlockSpec((None, d, tn), lambda i, j: (layer, 0, j + nb)),
        ],
        out_specs=pl.BlockSpec((tm, tn), lambda i, j: (i, j)),
        out_shape=jax.ShapeDtypeStruct((t, d), F32),
        scratch_shapes=[pltpu.VMEM((tm, d), BF16)],
        compiler_params=_params("parallel", "arbitrary"),
        name="conv_glu",
    )(x, g.reshape(1, d), w_in, w_in)


def _out_proj_kernel(x_ref, yp_ref, ys_ref, w_ref, o_ref, *scratch, n_pt):
    i = pl.program_id(0)
    if scratch:
        (wb_ref,) = scratch

        @pl.when(i == 0)
        def _():
            wb_ref[...] = w_ref[...].astype(BF16)
    else:
        wb_ref = w_ref

    def body(y_ref):
        for c0 in range(0, wb_ref.shape[1], PROJ_SUB):
            cols = slice(c0, c0 + PROJ_SUB)
            o_ref[:, cols] = x_ref[:, cols] + jnp.dot(y_ref[...], wb_ref[:, cols], preferred_element_type=F32)

    pl.when(i < n_pt)(lambda: body(yp_ref))
    pl.when(i >= n_pt)(lambda: body(ys_ref))


def _out_proj(x, y_p, y_s, w, layer):
    t, d = x.shape
    k = y_p.shape[1]
    tm = OUT_TM
    n_pt = y_p.shape[0] // tm
    row = lambda width: pl.BlockSpec((tm, width), lambda i: (i, 0))
    return pl.pallas_call(
        functools.partial(_out_proj_kernel, n_pt=n_pt),
        grid=(t // tm,),
        in_specs=[
            row(d),
            pl.BlockSpec((tm, k), lambda i: (jnp.minimum(i, n_pt - 1), 0)),
            pl.BlockSpec((tm, k), lambda i: (jnp.maximum(i - n_pt, 0), 0)),
            pl.BlockSpec((None, k, d), lambda i: (layer, 0, 0), pipeline_mode=pl.Buffered(1)),
        ],
        out_specs=row(d),
        out_shape=jax.ShapeDtypeStruct((t, d), F32),
        scratch_shapes=[] if w.dtype == BF16 else [pltpu.VMEM((k, d), BF16)],
        compiler_params=_params("arbitrary"),
        name="out_proj",
    )(x, y_p, y_s, w)


def _conv_taps(x, w_ref, cols, rows):
    off = CONV_PAD - CONV_HALO
    slab = x.shape[0]
    acc = jnp.zeros((rows, x.shape[1]), F32)
    for s in range(SUBLANES):
        xs = x if s == 0 else pltpu.roll(x, slab - s, 0)
        for a8 in range(0, CONV_PAD + SUBLANES, SUBLANES):
            w = a8 + s - off
            if 0 <= w < CONV_WIDTH:
                acc = acc + xs[a8:a8 + rows, :] * w_ref[w:w + 1, cols]
    return acc


def _glu_conv_kernel(x_ref, g_ref, w_ref, wdw_ref, b_ref, gn_ref, u_ref, z_ref, ext_ref, y_ref, *, tt):
    s = pl.program_id(0)
    d = x_ref.shape[-1]
    cur = s % 2
    prev = 1 - cur
    rows = CONV_ROWS
    slab = rows + CONV_PAD

    @pl.when(s == 0)
    def _():
        ext_ref[...] = jnp.zeros(ext_ref.shape, F32)

    h = _rmsnorm_rows(x_ref[...], g_ref[...]).astype(BF16)
    ext_ref[cur, 0:CONV_PAD, :] = ext_ref[prev, tt:tt + CONV_PAD, :]
    for c0 in range(0, d, PROJ_SUB):
        cols = slice(c0, c0 + PROJ_SUB)
        a = jnp.dot(h, w_ref[:, cols], preferred_element_type=F32)
        b = jnp.dot(h, w_ref[:, d + c0:d + c0 + PROJ_SUB], preferred_element_type=F32)
        u = a * jax.nn.sigmoid(b)
        u_ref[:, cols] = u
        ext_ref[cur, CONV_PAD:CONV_PAD + tt, cols] = u

    for r0 in range(0, tt, rows):
        for c0 in range(0, d, CONV_COLS):
            cols = slice(c0, c0 + CONV_COLS)
            acc = _conv_taps(ext_ref[prev, r0:r0 + slab, cols], wdw_ref, cols, rows)
            y_ref[r0:r0 + rows, cols] = acc + b_ref[:, cols]
    z_ref[...] = _silu(_rmsnorm_rows(y_ref[...], gn_ref[...])).astype(z_ref.dtype)


def _glu_conv_prompt(x, g, w_in, layer, n_rows, w_dw, b_dw, g_norm):
    d = x.shape[1]
    tt = CONV_TT
    nt = n_rows // tt
    tile = lambda s: jnp.minimum(s, nt - 1)
    vec = pl.BlockSpec((1, d), lambda s: (0, 0))
    return pl.pallas_call(
        functools.partial(_glu_conv_kernel, tt=tt),
        grid=(nt + 1,),
        in_specs=[
            pl.BlockSpec((tt, d), lambda s: (tile(s), 0)),
            vec,
            pl.BlockSpec((None, d, 2 * d), lambda s: (layer, 0, 0), pipeline_mode=pl.Buffered(1)),
            pl.BlockSpec((None, CONV_WIDTH, d), lambda s: (layer, 0, 0)),
            vec,
            vec,
        ],
        out_specs=[pl.BlockSpec((tt, d), lambda s: (tile(s), 0)),
                   pl.BlockSpec((tt, d), lambda s: (jnp.maximum(s - 1, 0), 0))],
        out_shape=[jax.ShapeDtypeStruct((n_rows, d), F32), jax.ShapeDtypeStruct((n_rows, d), BF16)],
        scratch_shapes=[pltpu.VMEM((2, CONV_PAD + tt, d), F32), pltpu.VMEM((tt, d), F32)],
        compiler_params=_params("arbitrary"),
        name="glu_conv_prompt",
    )(x, g.reshape(1, d), w_in, w_dw, b_dw.reshape(1, d), g_norm.reshape(1, d))


def _conv_sample_kernel(u_ref, halo_ref, w_ref, b_ref, gn_ref, z_ref, ext_ref, *, tt):
    d = u_ref.shape[-1]
    off = CONV_PAD - CONV_HALO
    ext_ref[0:SUBLANES, :] = jnp.zeros((SUBLANES, d), F32)
    ext_ref[off:CONV_PAD, :] = halo_ref[...]
    ext_ref[CONV_PAD:CONV_PAD + tt, :] = u_ref[...]
    parts = []
    for c0 in range(0, d, CONV_COLS):
        cols = slice(c0, c0 + CONV_COLS)
        parts.append(_conv_taps(ext_ref[:, cols], w_ref, cols, tt) + b_ref[:, cols])
    y = jnp.concatenate(parts, axis=1)
    z_ref[...] = _silu(_rmsnorm_rows(y, gn_ref[...])).astype(z_ref.dtype)


def _conv_sample(u, row0, cache, w_dw, b_dw, g_norm, layer):
    _, nb, _, d = cache.shape
    tt = (u.shape[0] - row0) // nb
    rb0 = row0 // tt
    kern = functools.partial(_conv_sample_kernel, tt=tt)
    return pl.pallas_call(
        kern,
        grid=(nb,),
        in_specs=[
            pl.BlockSpec((tt, d), lambda b: (b + rb0, 0)),
            pl.BlockSpec((None, None, CONV_HALO, d), lambda b: (layer, b, 0, 0)),
            pl.BlockSpec((None, CONV_WIDTH, d), lambda b: (layer, 0, 0)),
            pl.BlockSpec((1, d), lambda b: (0, 0)),
            pl.BlockSpec((1, d), lambda b: (0, 0)),
        ],
        out_specs=pl.BlockSpec((tt, d), lambda b: (b, 0)),
        out_shape=jax.ShapeDtypeStruct((nb * tt, d), BF16),
        scratch_shapes=[pltpu.VMEM((CONV_PAD + tt, d), F32)],
        compiler_params=_params("parallel"),
        name="conv_sample",
    )(u, cache, w_dw, b_dw.reshape(1, d), g_norm.reshape(1, d))


def _ret_kernel(lg_ref, gl_ref, q_ref, k_ref, v_ref, sg_ref, s0_ref, go_ref, o_ref, sout_ref, s_ref, *, L):
    hg = pl.program_id(1)
    c = pl.program_id(2)

    @pl.when(c == 0)
    def _():
        s_ref[...] = s0_ref[...]

    ri = lax.broadcasted_iota(jnp.int32, (L, L), 0)
    ci = lax.broadcasted_iota(jnp.int32, (L, L), 1)
    diff = (ri - ci).astype(F32)
    row = lax.broadcasted_iota(jnp.int32, (L, 1), 0).astype(F32)
    for hh in range(RET_HP):
        lg = lg_ref[hg * RET_HP + hh]
        g_chunk = gl_ref[hg * RET_HP + hh]
        kcols = slice(hh * RET_DK, (hh + 1) * RET_DK)
        vcols = slice(hh * RET_DV, (hh + 1) * RET_DV)
        q, k, v = q_ref[:, kcols], k_ref[:, kcols], v_ref[:, vcols]
        decay = jnp.where(diff >= 0, jnp.exp(lg * jnp.maximum(diff, 0.0)), 0.0)
        inner = lax.dot_general(q, k, (((1,), (1,)), ((), ())), preferred_element_type=F32) * decay
        o = jnp.dot(inner.astype(BF16), v, preferred_element_type=F32)

        state = s_ref[hh]
        q_dec = jnp.exp(lg * (row + 1.0))
        o = o + jnp.dot(q, state.astype(BF16), preferred_element_type=F32) * q_dec
        k_dec = jnp.exp(lg * (L - 1.0 - row))
        kd = (k.astype(F32) * k_dec).astype(BF16)
        new_state = state * g_chunk + lax.dot_general(kd, v, (((0,), (0,)), ((), ())),
                                                      preferred_element_type=F32)
        s_ref[hh] = new_state

        o = _rmsnorm_rows(o, go_ref[:, vcols])
        o_ref[:, vcols] = (sg_ref[:, vcols].astype(F32) * o).astype(o_ref.dtype)

    @pl.when(c == pl.num_programs(2) - 1)
    def _():
        sout_ref[...] = s_ref[...]


def _retention(qkvg, s0, layer, g_out, row0, n_seq, seq_len, L, decay_consts):
    nc = seq_len // L
    rb0 = row0 // L
    hp, dk, dv = RET_HP, RET_DK, RET_DV
    ng = RET_HEADS // hp
    row_map = lambda b, g, c: b * nc + c + rb0
    kern = functools.partial(_ret_kernel, L=L)
    lg, gl = decay_consts
    v0 = 2 * ng * dk // dv
    o, s_out = pl.pallas_call(
        kern,
        grid=(n_seq, ng, nc),
        in_specs=[
            pl.BlockSpec(memory_space=pltpu.SMEM),
            pl.BlockSpec(memory_space=pltpu.SMEM),
            pl.BlockSpec((L, hp * dk), lambda b, g, c: (row_map(b, g, c), g)),
            pl.BlockSpec((L, hp * dk), lambda b, g, c: (row_map(b, g, c), g + ng)),
            pl.BlockSpec((L, hp * dv), lambda b, g, c: (row_map(b, g, c), g + v0)),
            pl.BlockSpec((L, hp * dv), lambda b, g, c: (row_map(b, g, c), g + v0 + ng)),
            pl.BlockSpec((None, None, hp, dk, dv), lambda b, g, c: (layer, b, g, 0, 0)),
            pl.BlockSpec((1, hp * dv), lambda b, g, c: (0, g)),
        ],
        out_specs=[
            pl.BlockSpec((L, hp * dv), lambda b, g, c: (b * nc + c, g)),
            pl.BlockSpec((None, hp, dk, dv), lambda b, g, c: (b, g, 0, 0)),
        ],
        out_shape=[
            jax.ShapeDtypeStruct((n_seq * seq_len, RET_HEADS * dv), BF16),
            jax.ShapeDtypeStruct((n_seq, RET_HEADS, dk, dv), F32),
        ],
        scratch_shapes=[pltpu.VMEM((hp, dk, dv), F32)],
        compiler_params=_params("parallel", "parallel", "arbitrary"),
        name="retention",
    )(lg, gl, qkvg, qkvg, qkvg, qkvg, s0, g_out.reshape(1, RET_HEADS * dv))
    return o, s_out


def _softmax_weights(s):
    m = jnp.max(s, axis=-1, keepdims=True)
    p = jnp.exp(s - m)
    return p, jnp.sum(p, axis=-1, keepdims=True)


def _att_prompt_kernel(q_ref, k0_ref, k1_ref, k2_ref, v0_ref, v1_ref, v2_ref, bias_ref, o_ref):
    qb = pl.program_id(1)
    k_pos = qb * ATT_QB - BAND_PAST + lax.broadcasted_iota(jnp.int32, (1, ATT_KW), 1)
    pos_mask = jnp.where(k_pos >= 0, 0.0, MASK_VALUE).astype(F32)
    for hh in range(ATT_HP):
        cols = slice(hh * ATT_HD, (hh + 1) * ATT_HD)
        q = q_ref[:, cols]
        k = jnp.concatenate([k0_ref[:, cols], k1_ref[:, cols], k2_ref[:, cols]], axis=0)
        v = jnp.concatenate([v0_ref[:, cols], v1_ref[:, cols], v2_ref[:, cols]], axis=0)
        s = lax.dot_general(q, k, (((1,), (1,)), ((), ())), preferred_element_type=F32)
        s = s * (ATT_HD ** -0.5) + bias_ref[hh] + pos_mask
        p, l = _softmax_weights(s)
        o = jnp.dot(p.astype(BF16), v, preferred_element_type=F32)
        o_ref[:, cols] = (o / l).astype(o_ref.dtype)


def _att_prompt(qkv, bias, n_rows):
    nqb = n_rows // ATT_QB
    nw = ATT_KW // ATT_QB
    ng = ATT_HEADS // ATT_HP
    blk = (ATT_QB, ATT_HP * ATT_HD)

    def kv_spec(back, section):
        return pl.BlockSpec(blk, lambda g, i: (jnp.maximum(i - back, 0), g + section * ng))

    k_specs = [kv_spec(nw - 1 - n, 1) for n in range(nw)]
    v_specs = [kv_spec(nw - 1 - n, 2) for n in range(nw)]
    return pl.pallas_call(
        _att_prompt_kernel,
        grid=(ng, nqb),
        in_specs=[pl.BlockSpec(blk, lambda g, i: (i, g))] + k_specs + v_specs + [
            pl.BlockSpec((ATT_HP, ATT_QB, ATT_KW), lambda g, i: (g, 0, 0)),
        ],
        out_specs=pl.BlockSpec(blk, lambda g, i: (i, g)),
        out_shape=jax.ShapeDtypeStruct((n_rows, ATT_HEADS * ATT_HD), BF16),
        compiler_params=_params("parallel", "arbitrary"),
        name="att_prompt",
    )(*([qkv] * (1 + 2 * nw)), bias)


def _att_sample_kernel(q_ref, kn_ref, vn_ref, kc_ref, vc_ref, bc_ref, bn_ref, o_ref, kd_ref, vd_ref):
    scale = ATT_HD ** -0.5
    dims = (((1,), (1,)), ((), ()))
    lc = kc_ref.shape[0] // ATT_HEADS
    kd_ref[...] = pltpu.einshape("(th)d->htd", kc_ref[...], h=ATT_HEADS).astype(BF16)
    vd_ref[...] = pltpu.einshape("(th)d->htd", vc_ref[...], h=ATT_HEADS).astype(BF16)
    for h in range(ATT_HEADS):
        cols = slice(h * ATT_HD, (h + 1) * ATT_HD)
        q = q_ref[:, cols]
        kc = kd_ref[h]
        vc = vd_ref[h]
        sc = lax.dot_general(q, kc, dims, preferred_element_type=F32) * scale + bc_ref[h]
        sn = lax.dot_general(q, kn_ref[:, cols], dims, preferred_element_type=F32) * scale + bn_ref[h]
        m = jnp.maximum(jnp.max(sc, axis=-1, keepdims=True), jnp.max(sn, axis=-1, keepdims=True))
        pc = jnp.exp(sc - m)
        pn = jnp.exp(sn - m)
        l = jnp.sum(pc, axis=-1, keepdims=True) + jnp.sum(pn, axis=-1, keepdims=True)
        o = jnp.dot(pc.astype(BF16), vc, preferred_element_type=F32)
        o = o + jnp.dot(pn.astype(BF16), vn_ref[:, cols], preferred_element_type=F32)
        o_ref[:, cols] = (o / l).astype(o_ref.dtype)


def _att_sample(qkv, row0, cache_k, cache_v, layer, bias_c, bias_n):
    _, nb, lc, nh, hd = cache_k.shape
    d = nh * hd
    ts = (qkv.shape[0] - row0) // nb
    rb0 = row0 // ts
    new_specs = [pl.BlockSpec((ts, d), lambda b, section=section: (b + rb0, section)) for section in range(3)]
    cache_k = cache_k.reshape(cache_k.shape[0], nb, lc * nh, hd)
    cache_v = cache_v.reshape(cache_v.shape[0], nb, lc * nh, hd)
    cache_spec = pl.BlockSpec((None, None, lc * nh, hd), lambda b: (layer, b, 0, 0))
    return pl.pallas_call(
        _att_sample_kernel,
        grid=(nb,),
        in_specs=new_specs + [cache_spec, cache_spec,
                              pl.BlockSpec((nh, ts, lc), lambda b: (0, 0, 0)),
                              pl.BlockSpec((nh, ts, ts), lambda b: (0, 0, 0))],
        out_specs=pl.BlockSpec((ts, d), lambda b: (b, 0)),
        out_shape=jax.ShapeDtypeStruct((nb * ts, d), BF16),
        scratch_shapes=[pltpu.VMEM((nh, lc, hd), BF16), pltpu.VMEM((nh, lc, hd), BF16)],
        compiler_params=_params("parallel"),
        name="att_sample",
    )(qkv, qkv, qkv, cache_k, cache_v, bias_c, bias_n)


def _band_bias(rel_table, nq, nk, chunk_band):
    nh = rel_table.shape[0]
    n = nq + nk - 1
    q_minus_k = nq - 1 - jnp.arange(n, dtype=jnp.int32) + BAND_PAST
    r = rel_table[:, jnp.clip(q_minus_k, -REL_CLIP, REL_CLIP) + REL_CLIP].astype(F32)
    r = jnp.pad(r, ((0, 0), (0, 1)))
    skew = jnp.tile(r, (1, nq))[:, :nq * n].reshape(nh, nq, n)
    bias = skew[:, :, nq - 1:nq - 1 + nk]
    if chunk_band:
        qc = jnp.arange(nq, dtype=jnp.int32)[:, None] // CHUNK
        kc = jnp.arange(nk, dtype=jnp.int32)[None, :] // CHUNK
        visible = (kc >= qc) & (kc <= qc + N_PREV_CHUNKS)
        bias = jnp.where(visible[None], bias, MASK_VALUE)
    return bias


def kernel(x_prompt, x_sample, cache_conv, state_ret, cache_att_k, cache_att_v, g_ffn1, w_ffn1_gu, w_ffn1_down, g_mix, g_ffn2, w_ffn2_gu, w_ffn2_down, w_conv_in, w_conv_dw, b_conv_dw, g_conv_norm, w_conv_out, w_ret_in, g_ret_out, w_ret_out, w_att_qkv, g_att_q, g_att_k, att_rel_bias, w_att_out):
    bp, tp, d = x_prompt.shape
    bs, ts, _ = x_sample.shape
    depth = g_mix.shape[0]
    n_p = bp * tp
    n_s = bs * ts
    assert bp == 1, "the prompt path carries conv / retention state along one sequence"

    ffn_stacks = ((w_ffn1_gu, w_ffn1_down), (w_ffn2_gu, w_ffn2_down))
    ffn_w = (w_ffn1_gu[0].astype(BF16), w_ffn1_down[0].astype(BF16))
    w_ret_out = w_ret_out.astype(BF16)
    w_conv_in = w_conv_in.astype(BF16)

    pos = jnp.concatenate([jnp.arange(tp, dtype=jnp.int32),
                           jnp.tile(PAST_LEN + jnp.arange(ts, dtype=jnp.int32), bs)])
    half = RET_DK // 2
    freq = ROPE_BASE ** (-jnp.arange(half, dtype=F32) / half)
    ang = pos.astype(F32)[:, None] * freq[None, :]
    rope = (jnp.cos(ang), jnp.sin(ang))
    log_g = jnp.log1p(-jnp.exp2(-5.0 - jnp.arange(RET_HEADS, dtype=F32)))
    l_p = RET_L if tp % RET_L == 0 else tp
    l_s = ts

    conv_p, conv_s, ret_p, ret_s = [], [], [], []
    kp_l, vp_l, ks_l, vs_l = [], [], [], []
    xs = [x_prompt.reshape(n_p, d), x_sample.reshape(n_s, d)]
    for i in range(depth):
        kind, j = i % N_MIXERS, i // N_MIXERS
        x, *ffn_w = _ffn(xs, g_ffn1[i], *ffn_w, n_p, next_weights=ffn_stacks[1] + (i,))
        if kind == 0:
            u_p, z_p = _glu_conv_prompt(x, g_mix[i], w_conv_in, j, n_p, w_conv_dw, b_conv_dw[j], g_conv_norm[j])
            u_s = _glu(x, g_mix[i], w_conv_in, j, n_p)
            z_s = _conv_sample(u_s, 0, cache_conv, w_conv_dw, b_conv_dw[j], g_conv_norm[j], j)
            x = _out_proj(x, z_p, z_s, w_conv_out, j)
            conv_p.append(u_p[n_p - CONV_HALO:].reshape(bp, CONV_HALO, d))
            conv_s.append(u_s.reshape(bs, ts, d)[:, ts - CONV_HALO:])
        elif kind == 1:
            nq_t = RET_HEADS * RET_DK // PROJ_TN
            nv_t = RET_HEADS * RET_DV // PROJ_TN
            segments = [(nq_t, "rope_q", 0, False), (nq_t, "rope_k", 0, False),
                        (nv_t, "plain", 0, False), (nv_t, "silu", 0, False)]
            (qkvg,) = _proj(x, g_mix[i], w_ret_in, j, segments, rope=rope)
            zero_state = jnp.zeros((1, bp, RET_HEADS, RET_DK, RET_DV), F32)
            o_p, s_p = _retention(qkvg, zero_state, 0, g_ret_out[j], 0, bp, tp, l_p,
                                  (log_g, jnp.exp(log_g * l_p)))
            o_s, s_s = _retention(qkvg, state_ret, j, g_ret_out[j], n_p, bs, ts, l_s,
                                  (log_g, jnp.exp(log_g * l_s)))
            x = _out_proj(x, o_p, o_s, w_ret_out, j)
            ret_p.append(s_p)
            ret_s.append(s_s)
        else:
            nd_t = d // PROJ_TN
            segments = [(nd_t, "headnorm", 0, False), (nd_t, "headnorm", 1, True), (nd_t, "plain", 0, True)]
            qkv, kv32 = _proj(x, g_mix[i], w_att_qkv, j, segments, gains=jnp.stack([g_att_q[j], g_att_k[j]]))
            k32, v32 = kv32[:, :d], kv32[:, d:]
            lc = cache_att_k.shape[2]
            o_p = _att_prompt(qkv, _band_bias(att_rel_bias[j], ATT_QB, ATT_KW, True), n_p)
            bias_s = _band_bias(att_rel_bias[j], ts, lc + ts, False)
            o_s = _att_sample(qkv, n_p, cache_att_k, cache_att_v, j, bias_s[:, :, :lc], bias_s[:, :, lc:])
            x = _out_proj(x, o_p, o_s, w_att_out, j)
            keep = min(BAND_PAST, tp)
            kp_l.append(k32[n_p - keep:n_p].reshape(bp, keep, ATT_HEADS, ATT_HD))
            vp_l.append(v32[n_p - keep:n_p].reshape(bp, keep, ATT_HEADS, ATT_HD))
            ks_l.append(k32[n_p:].reshape(bs, ts, ATT_HEADS, ATT_HD))
            vs_l.append(v32[n_p:].reshape(bs, ts, ATT_HEADS, ATT_HD))
        last = i == depth - 1
        xs = _ffn([x], g_ffn2[i], *ffn_w, n_p, split_out=last,
                  next_weights=None if last else ffn_stacks[0] + (i + 1,))
        if not last:
            xs, ffn_w = xs[:1], xs[1:]

    return (xs[0].reshape(bp, tp, d), xs[1].reshape(bs, ts, d),
            jnp.stack(conv_p), jnp.stack(conv_s),
            jnp.stack(ret_p), jnp.stack(ret_s),
            jnp.stack(kp_l), jnp.stack(vp_l),
            jnp.stack(ks_l), jnp.stack(vs_l))
```

```python
import functools

import jax
import jax.numpy as jnp
from jax import lax
from jax.experimental import pallas as pl
from jax.experimental.pallas import tpu as pltpu

F32 = jnp.float32
BF16 = jnp.bfloat16

CHUNK = 64
NORM_EPS = 1e-6
CONV_WIDTH = 31
CONV_HALO = CONV_WIDTH - 1
RET_HEADS = 8
RET_DK = 256
RET_DV = 512
ROPE_BASE = 10000.0
ATT_HEADS = 16
ATT_HD = 128
N_PREV_CHUNKS = 8
BAND_PAST = N_PREV_CHUNKS * CHUNK
REL_CLIP = 256
PAST_LEN = 4096
N_MIXERS = 3

LANES = 128
SUBLANES = 8
VMEM_BYTES = 64 * 1024 * 1024
VMEM_LIMIT_BYTES = 56 * 1024 * 1024
FFN_VMEM_LIMIT_BYTES = 60 * 1024 * 1024
assert FFN_VMEM_LIMIT_BYTES < VMEM_BYTES
MASK_VALUE = -1e30

FFN_TM = 1024
FFN_TM_SPLIT = 512
FFN_TF = 512
FFN_CAST_BLOCKS = 8
PROJ_TM = 1024
PROJ_TN = 1024
PROJ_SUB = 512
GLU_TN = 512
GLU_SUB = 256
OUT_TM = 512
CONV_TT = 256
CONV_ROWS = 64
CONV_COLS = 128
CONV_PAD = 32
RET_L = 256
RET_HP = 4
ATT_QB = 256
ATT_KW = ATT_QB + BAND_PAST
ATT_HP = 8


def _params(*sem, vmem_limit_bytes=VMEM_LIMIT_BYTES):
    return pltpu.CompilerParams(dimension_semantics=sem, vmem_limit_bytes=vmem_limit_bytes)


def _as_bf16(w):
    return w if w.dtype == BF16 else w.astype(BF16)


def _rmsnorm_rows(x, g):
    ms = jnp.mean(x * x, axis=-1, keepdims=True)
    return x * lax.rsqrt(ms + NORM_EPS) * g


def _silu(x):
    return x * jax.nn.sigmoid(x)


def _row_tile_specs(tm, width, n_pt):
    whole = pl.BlockSpec((tm, width), lambda i, j: (i, 0))
    prompt = pl.BlockSpec((tm, width), lambda i, j: (jnp.minimum(i, n_pt - 1), 0))
    sample = pl.BlockSpec((tm, width), lambda i, j: (jnp.maximum(i - n_pt, 0), 0))
    return whole, prompt, sample


def _ffn_kernel(*refs, n_pt, split_in, split_out, cast_next):
    refs = list(refs)
    x_refs = [refs.pop(0) for _ in range(2 if split_in else 1)]
    g_ref, wg_ref, wu_ref, wd_ref = refs[:4]
    refs = refs[4:]
    h_ref = refs.pop()
    if cast_next:
        next_gu_ref, next_down_ref = refs[:2]
        cast_gu_ref, cast_down_ref = refs[-2:]
        refs = refs[2:-2]
    o_refs = refs
    i = pl.program_id(0)
    j = pl.program_id(1)

    def body(x_ref, o_ref):
        @pl.when(j == 0)
        def _():
            x = x_ref[...]
            h_ref[...] = _rmsnorm_rows(x, g_ref[...]).astype(BF16)
            o_ref[...] = x

        h = h_ref[...]
        gate = jnp.dot(h, wg_ref[...], preferred_element_type=F32)
        up = jnp.dot(h, wu_ref[...], preferred_element_type=F32)
        a = (_silu(gate) * (0.5 * up)).astype(BF16)
        o_ref[...] += jnp.dot(a, wd_ref[...], preferred_element_type=F32)

    if not (split_in or split_out):
        body(x_refs[0], o_refs[0])
    else:
        pl.when(i < n_pt)(lambda: body(x_refs[0], o_refs[0]))
        pl.when(i >= n_pt)(lambda: body(x_refs[-1], o_refs[-1]))

    if cast_next:
        @pl.when(i < FFN_CAST_BLOCKS)
        def _():
            cast_gu_ref[...] = next_gu_ref[...].astype(BF16)
            cast_down_ref[...] = next_down_ref[...].astype(BF16)


def _ffn(xs, g, w_gu, w_down, n_p, split_out=False, next_weights=None):
    split_in = len(xs) == 2
    d = xs[0].shape[1]
    t = sum(x.shape[0] for x in xs)
    f = w_down.shape[0]
    nf = f // FFN_TF
    tm = FFN_TM_SPLIT if split_out else FFN_TM
    whole, prompt, sample = _row_tile_specs(tm, d, n_p // tm)
    n_pt = n_p // tm
    in_prompt = pl.BlockSpec((tm, d), lambda i, j: (jnp.minimum(i, n_pt - 1), 0), pipeline_mode=pl.Buffered(1))
    in_sample = pl.BlockSpec((tm, d), lambda i, j: (jnp.maximum(i - n_pt, 0), 0), pipeline_mode=pl.Buffered(1))
    in_specs = ([in_prompt, in_sample] if split_in else [whole]) + [
        pl.BlockSpec((1, d), lambda i, j: (0, 0)),
        pl.BlockSpec((d, FFN_TF), lambda i, j: (0, j)),
        pl.BlockSpec((d, FFN_TF), lambda i, j: (0, j + nf)),
        pl.BlockSpec((FFN_TF, d), lambda i, j: (j, 0)),
    ]
    args = list(xs) + [g.reshape(1, d), w_gu, w_gu, w_down]
    out_specs = [prompt, sample] if split_out else [whole]
    out_shape = ([jax.ShapeDtypeStruct((n_p, d), F32), jax.ShapeDtypeStruct((t - n_p, d), F32)] if split_out
                 else [jax.ShapeDtypeStruct((t, d), F32)])
    if next_weights is not None:
        gu_stack, down_stack, layer = next_weights
        nb = FFN_CAST_BLOCKS
        assert t // tm >= nb
        bi = lambda i, j: jnp.where(i < nb, i, nb - 1)
        bj = lambda i, j: jnp.where(i < nb, j, nf - 1)
        gu_blk, down_blk = (d // nb, 2 * f // nf), (f // nf, d // nb)
        in_specs += [pl.BlockSpec((None,) + gu_blk, lambda i, j: (layer, bi(i, j), bj(i, j))),
                     pl.BlockSpec((None,) + down_blk, lambda i, j: (layer, bj(i, j), bi(i, j)))]
        args += [gu_stack, down_stack]
        out_specs += [pl.BlockSpec(gu_blk, lambda i, j: (bi(i, j), bj(i, j))),
                      pl.BlockSpec(down_blk, lambda i, j: (bj(i, j), bi(i, j)))]
        out_shape += [jax.ShapeDtypeStruct((d, 2 * f), BF16), jax.ShapeDtypeStruct((f, d), BF16)]
    kern = functools.partial(_ffn_kernel, n_pt=n_p // tm, split_in=split_in, split_out=split_out,
                             cast_next=next_weights is not None)
    return pl.pallas_call(
        kern,
        grid=(t // tm, nf),
        in_specs=in_specs,
        out_specs=out_specs,
        out_shape=out_shape,
        scratch_shapes=[pltpu.VMEM((tm, d), BF16)],
        compiler_params=_params("arbitrary", "arbitrary", vmem_limit_bytes=FFN_VMEM_LIMIT_BYTES),
        name="ffn",
    )(*args)


def _proj_kernel(*refs, segments, has_rope, has_gain, has_f32):
    it = iter(refs)
    x_ref, g_ref, w_ref = next(it), next(it), next(it)
    cos_ref = sin_ref = gh_ref = o32_ref = None
    if has_rope:
        cos_ref, sin_ref = next(it), next(it)
    if has_gain:
        gh_ref = next(it)
    o_ref = next(it)
    if has_f32:
        o32_ref = next(it)
    h_ref = next(it)
    j = pl.program_id(1)

    @pl.when(j == 0)
    def _():
        h_ref[...] = _rmsnorm_rows(x_ref[...], g_ref[...]).astype(BF16)

    def epilogue(y, mode, gain_row):
        if mode == "silu":
            return _silu(y)
        if mode in ("rope_q", "rope_k"):
            cos, sin = cos_ref[...], sin_ref[...]
            scale = RET_DK ** -0.5 if mode == "rope_k" else 1.0
            half = RET_DK // 2
            parts = []
            for hh in range(y.shape[1] // RET_DK):
                x1 = y[:, hh * RET_DK: hh * RET_DK + half]
                x2 = y[:, hh * RET_DK + half: (hh + 1) * RET_DK]
                parts.append((x1 * cos - x2 * sin) * scale)
                parts.append((x1 * sin + x2 * cos) * scale)
            return jnp.concatenate(parts, axis=1)
        if mode == "headnorm":
            gh = gh_ref[gain_row:gain_row + 1, :]
            return jnp.concatenate([_rmsnorm_rows(y[:, hh * ATT_HD:(hh + 1) * ATT_HD], gh)
                                    for hh in range(y.shape[1] // ATT_HD)], axis=1)
        return y

    def column_tile(mode, gain_row, f32_copy):
        for c0 in range(0, w_ref.shape[1], PROJ_SUB):
            cols = slice(c0, c0 + PROJ_SUB)
            y = jnp.dot(h_ref[...], _as_bf16(w_ref[:, cols]), preferred_element_type=F32)
            z = epilogue(y, mode, gain_row)
            o_ref[:, cols] = z.astype(o_ref.dtype)
            if f32_copy:
                o32_ref[:, cols] = z

    j0 = 0
    for n_tiles, mode, gain_row, f32_copy in segments:
        in_segment = jnp.logical_and(j >= j0, j < j0 + n_tiles)
        pl.when(in_segment)(functools.partial(column_tile, mode, gain_row, f32_copy))
        j0 += n_tiles


def _proj(x, g, w, layer, segments, *, rope=None, gains=None):
    t, d = x.shape
    tm, tn = PROJ_TM, PROJ_TN
    n_tiles = sum(seg[0] for seg in segments)
    n32_tiles = sum(seg[0] for seg in segments if seg[3])
    j32 = n_tiles - n32_tiles
    in_specs = [
        pl.BlockSpec((tm, d), lambda i, j: (i, 0)),
        pl.BlockSpec((1, d), lambda i, j: (0, 0)),
        pl.BlockSpec((None, d, tn), lambda i, j: (layer, 0, j)),
    ]
    args = [x, g.reshape(1, d), w]
    if rope is not None:
        in_specs += [pl.BlockSpec((tm, RET_DK // 2), lambda i, j: (i, 0))] * 2
        args += list(rope)
    if gains is not None:
        in_specs.append(pl.BlockSpec(gains.shape, lambda i, j: (0, 0)))
        args.append(gains)
    out_specs = [pl.BlockSpec((tm, tn), lambda i, j: (i, j))]
    out_shape = [jax.ShapeDtypeStruct((t, n_tiles * tn), BF16)]
    if n32_tiles:
        out_specs.append(pl.BlockSpec((tm, tn), lambda i, j: (i, jnp.maximum(j - j32, 0))))
        out_shape.append(jax.ShapeDtypeStruct((t, n32_tiles * tn), F32))
    kern = functools.partial(_proj_kernel, segments=tuple(segments), has_rope=rope is not None,
                             has_gain=gains is not None, has_f32=bool(n32_tiles))
    return pl.pallas_call(
        kern,
        grid=(t // tm, n_tiles),
        in_specs=in_specs,
        out_specs=out_specs,
        out_shape=out_shape,
        scratch_shapes=[pltpu.VMEM((tm, d), BF16)],
        compiler_params=_params("arbitrary", "arbitrary"),
        name="proj",
    )(*args)


def _glu_kernel(x_ref, g_ref, wa_ref, wb_ref, o_ref, h_ref):
    @pl.when(pl.program_id(1) == 0)
    def _():
        h_ref[...] = _rmsnorm_rows(x_ref[...], g_ref[...]).astype(BF16)

    h = h_ref[...]
    a = jnp.dot(h, _as_bf16(wa_ref[...]), preferred_element_type=F32)
    b = jnp.dot(h, _as_bf16(wb_ref[...]), preferred_element_type=F32)
    o_ref[...] = a * jax.nn.sigmoid(b)


def _glu(x, g, w_in, layer, row0):
    d = x.shape[1]
    t = x.shape[0] - row0
    tm, tn = PROJ_TM, GLU_TN
    nb = d // tn
    rb0 = row0 // tm
    return pl.pallas_call(
        _glu_kernel,
        grid=(t // tm, nb),
        in_specs=[
            pl.BlockSpec((tm, d), lambda i, j: (i + rb0, 0)),
            pl.BlockSpec((1, d), lambda i, j: (0, 0)),
            pl.BlockSpec((None, d, tn), lambda i, j: (layer, 0, j)),
            pl.BlockSpec((None, d, tn), lambda i, j: (layer, 0, j + nb)),
        ],
        out_specs=pl.BlockSpec((tm, tn), lambda i, j: (i, j)),
        out_shape=jax.ShapeDtypeStruct((t, d), F32),
        scratch_shapes=[pltpu.VMEM((tm, d), BF16)],
        compiler_params=_params("parallel", "arbitrary"),
        name="conv_glu",
    )(x, g.reshape(1, d), w_in, w_in)


def _out_proj_kernel(x_ref, yp_ref, ys_ref, w_ref, o_ref, *scratch, n_pt):
    i = pl.program_id(0)
    if scratch:
        (wb_ref,) = scratch

        @pl.when(i == 0)
        def _():
            wb_ref[...] = w_ref[...].astype(BF16)
    else:
        wb_ref = w_ref

    def body(y_ref):
        for c0 in range(0, wb_ref.shape[1], PROJ_SUB):
            cols = slice(c0, c0 + PROJ_SUB)
            o_ref[:, cols] = x_ref[:, cols] + jnp.dot(y_ref[...], wb_ref[:, cols], preferred_element_type=F32)

    pl.when(i < n_pt)(lambda: body(yp_ref))
    pl.when(i >= n_pt)(lambda: body(ys_ref))


def _out_proj(x, y_p, y_s, w, layer):
    t, d = x.shape
    k = y_p.shape[1]
    tm = OUT_TM
    n_pt = y_p.shape[0] // tm
    row = lambda width: pl.BlockSpec((tm, width), lambda i: (i, 0))
    return pl.pallas_call(
        functools.partial(_out_proj_kernel, n_pt=n_pt),
        grid=(t // tm,),
        in_specs=[
            row(d),
            pl.BlockSpec((tm, k), lambda i: (jnp.minimum(i, n_pt - 1), 0)),
            pl.BlockSpec((tm, k), lambda i: (jnp.maximum(i - n_pt, 0), 0)),
            pl.BlockSpec((None, k, d), lambda i: (layer, 0, 0), pipeline_mode=pl.Buffered(1)),
        ],
        out_specs=row(d),
        out_shape=jax.ShapeDtypeStruct((t, d), F32),
        scratch_shapes=[] if w.dtype == BF16 else [pltpu.VMEM((k, d), BF16)],
        compiler_params=_params("arbitrary"),
        name="out_proj",
    )(x, y_p, y_s, w)


def _conv_taps(x, w_ref, cols, rows):
    off = CONV_PAD - CONV_HALO
    slab = x.shape[0]
    acc = jnp.zeros((rows, x.shape[1]), F32)
    for s in range(SUBLANES):
        xs = x if s == 0 else pltpu.roll(x, slab - s, 0)
        for a8 in range(0, CONV_PAD + SUBLANES, SUBLANES):
            w = a8 + s - off
            if 0 <= w < CONV_WIDTH:
                acc = acc + xs[a8:a8 + rows, :] * w_ref[w:w + 1, cols]
    return acc


def _glu_conv_kernel(x_ref, g_ref, w_ref, wdw_ref, b_ref, gn_ref, u_ref, z_ref, ext_ref, y_ref, *, tt):
    s = pl.program_id(0)
    d = x_ref.shape[-1]
    cur = s % 2
    prev = 1 - cur
    rows = CONV_ROWS
    slab = rows + CONV_PAD

    @pl.when(s == 0)
    def _():
        ext_ref[...] = jnp.zeros(ext_ref.shape, F32)

    h = _rmsnorm_rows(x_ref[...], g_ref[...]).astype(BF16)
    ext_ref[cur, 0:CONV_PAD, :] = ext_ref[prev, tt:tt + CONV_PAD, :]
    for c0 in range(0, d, GLU_SUB):
        cols = slice(c0, c0 + GLU_SUB)
        a = jnp.dot(h, w_ref[:, cols], preferred_element_type=F32)
        b = jnp.dot(h, w_ref[:, d + c0:d + c0 + GLU_SUB], preferred_element_type=F32)
        u = a * jax.nn.sigmoid(b)
        u_ref[:, cols] = u
        ext_ref[cur, CONV_PAD:CONV_PAD + tt, cols] = u

    for r0 in range(0, tt, rows):
        for c0 in range(0, d, CONV_COLS):
            cols = slice(c0, c0 + CONV_COLS)
            acc = _conv_taps(ext_ref[prev, r0:r0 + slab, cols], wdw_ref, cols, rows)
            y_ref[r0:r0 + rows, cols] = acc + b_ref[:, cols]
    z_ref[...] = _silu(_rmsnorm_rows(y_ref[...], gn_ref[...])).astype(z_ref.dtype)


def _glu_conv_prompt(x, g, w_in, layer, n_rows, w_dw, b_dw, g_norm):
    d = x.shape[1]
    tt = CONV_TT
    nt = n_rows // tt
    tile = lambda s: jnp.minimum(s, nt - 1)
    vec = pl.BlockSpec((1, d), lambda s: (0, 0))
    return pl.pallas_call(
        functools.partial(_glu_conv_kernel, tt=tt),
        grid=(nt + 1,),
        in_specs=[
            pl.BlockSpec((tt, d), lambda s: (tile(s), 0)),
            vec,
            pl.BlockSpec((None, d, 2 * d), lambda s: (layer, 0, 0), pipeline_mode=pl.Buffered(1)),
            pl.BlockSpec((None, CONV_WIDTH, d), lambda s: (layer, 0, 0)),
            vec,
            vec,
        ],
        out_specs=[pl.BlockSpec((tt, d), lambda s: (tile(s), 0)),
                   pl.BlockSpec((tt, d), lambda s: (jnp.maximum(s - 1, 0), 0))],
        out_shape=[jax.ShapeDtypeStruct((n_rows, d), F32), jax.ShapeDtypeStruct((n_rows, d), BF16)],
        scratch_shapes=[pltpu.VMEM((2, CONV_PAD + tt, d), F32), pltpu.VMEM((tt, d), F32)],
        compiler_params=_params("arbitrary"),
        name="glu_conv_prompt",
    )(x, g.reshape(1, d), w_in, w_dw, b_dw.reshape(1, d), g_norm.reshape(1, d))


def _conv_sample_kernel(u_ref, halo_ref, w_ref, b_ref, gn_ref, z_ref, ext_ref, *, tt):
    d = u_ref.shape[-1]
    off = CONV_PAD - CONV_HALO
    ext_ref[0:SUBLANES, :] = jnp.zeros((SUBLANES, d), F32)
    ext_ref[off:CONV_PAD, :] = halo_ref[...]
    ext_ref[CONV_PAD:CONV_PAD + tt, :] = u_ref[...]
    parts = []
    for c0 in range(0, d, CONV_COLS):
        cols = slice(c0, c0 + CONV_COLS)
        parts.append(_conv_taps(ext_ref[:, cols], w_ref, cols, tt) + b_ref[:, cols])
    y = jnp.concatenate(parts, axis=1)
    z_ref[...] = _silu(_rmsnorm_rows(y, gn_ref[...])).astype(z_ref.dtype)


def _conv_sample(u, row0, cache, w_dw, b_dw, g_norm, layer):
    _, nb, _, d = cache.shape
    tt = (u.shape[0] - row0) // nb
    rb0 = row0 // tt
    kern = functools.partial(_conv_sample_kernel, tt=tt)
    return pl.pallas_call(
        kern,
        grid=(nb,),
        in_specs=[
            pl.BlockSpec((tt, d), lambda b: (b + rb0, 0)),
            pl.BlockSpec((None, None, CONV_HALO, d), lambda b: (layer, b, 0, 0)),
            pl.BlockSpec((None, CONV_WIDTH, d), lambda b: (layer, 0, 0)),
            pl.BlockSpec((1, d), lambda b: (0, 0)),
            pl.BlockSpec((1, d), lambda b: (0, 0)),
        ],
        out_specs=pl.BlockSpec((tt, d), lambda b: (b, 0)),
        out_shape=jax.ShapeDtypeStruct((nb * tt, d), BF16),
        scratch_shapes=[pltpu.VMEM((CONV_PAD + tt, d), F32)],
        compiler_params=_params("parallel"),
        name="conv_sample",
    )(u, cache, w_dw, b_dw.reshape(1, d), g_norm.reshape(1, d))


def _ret_kernel(lg_ref, gl_ref, q_ref, k_ref, v_ref, sg_ref, s0_ref, go_ref, o_ref, sout_ref, s_ref, *, L):
    hg = pl.program_id(1)
    c = pl.program_id(2)

    @pl.when(c == 0)
    def _():
        s_ref[...] = s0_ref[...]

    ri = lax.broadcasted_iota(jnp.int32, (L, L), 0)
    ci = lax.broadcasted_iota(jnp.int32, (L, L), 1)
    diff = (ri - ci).astype(F32)
    row = lax.broadcasted_iota(jnp.int32, (L, 1), 0).astype(F32)
    for hh in range(RET_HP):
        lg = lg_ref[hg * RET_HP + hh]
        g_chunk = gl_ref[hg * RET_HP + hh]
        kcols = slice(hh * RET_DK, (hh + 1) * RET_DK)
        vcols = slice(hh * RET_DV, (hh + 1) * RET_DV)
        q, k, v = q_ref[:, kcols], k_ref[:, kcols], v_ref[:, vcols]
        decay = jnp.where(diff >= 0, jnp.exp(lg * jnp.maximum(diff, 0.0)), 0.0)
        inner = lax.dot_general(q, k, (((1,), (1,)), ((), ())), preferred_element_type=F32) * decay
        o = jnp.dot(inner.astype(BF16), v, preferred_element_type=F32)

        state = s_ref[hh]
        q_dec = jnp.exp(lg * (row + 1.0))
        o = o + jnp.dot(q, state.astype(BF16), preferred_element_type=F32) * q_dec
        k_dec = jnp.exp(lg * (L - 1.0 - row))
        kd = (k.astype(F32) * k_dec).astype(BF16)
        new_state = state * g_chunk + lax.dot_general(kd, v, (((0,), (0,)), ((), ())),
                                                      preferred_element_type=F32)
        s_ref[hh] = new_state

        o = _rmsnorm_rows(o, go_ref[:, vcols])
        o_ref[:, vcols] = (sg_ref[:, vcols].astype(F32) * o).astype(o_ref.dtype)

    @pl.when(c == pl.num_programs(2) - 1)
    def _():
        sout_ref[...] = s_ref[...]


def _retention(qkvg, s0, layer, g_out, row0, n_seq, seq_len, L, decay_consts):
    nc = seq_len // L
    rb0 = row0 // L
    hp, dk, dv = RET_HP, RET_DK, RET_DV
    ng = RET_HEADS // hp
    row_map = lambda b, g, c: b * nc + c + rb0
    kern = functools.partial(_ret_kernel, L=L)
    lg, gl = decay_consts
    v0 = 2 * ng * dk // dv
    o, s_out = pl.pallas_call(
        kern,
        grid=(n_seq, ng, nc),
        in_specs=[
            pl.BlockSpec(memory_space=pltpu.SMEM),
            pl.BlockSpec(memory_space=pltpu.SMEM),
            pl.BlockSpec((L, hp * dk), lambda b, g, c: (row_map(b, g, c), g)),
            pl.BlockSpec((L, hp * dk), lambda b, g, c: (row_map(b, g, c), g + ng)),
            pl.BlockSpec((L, hp * dv), lambda b, g, c: (row_map(b, g, c), g + v0)),
            pl.BlockSpec((L, hp * dv), lambda b, g, c: (row_map(b, g, c), g + v0 + ng)),
            pl.BlockSpec((None, None, hp, dk, dv), lambda b, g, c: (layer, b, g, 0, 0)),
            pl.BlockSpec((1, hp * dv), lambda b, g, c: (0, g)),
        ],
        out_specs=[
            pl.BlockSpec((L, hp * dv), lambda b, g, c: (b * nc + c, g)),
            pl.BlockSpec((None, hp, dk, dv), lambda b, g, c: (b, g, 0, 0)),
        ],
        out_shape=[
            jax.ShapeDtypeStruct((n_seq * seq_len, RET_HEADS * dv), BF16),
            jax.ShapeDtypeStruct((n_seq, RET_HEADS, dk, dv), F32),
        ],
        scratch_shapes=[pltpu.VMEM((hp, dk, dv), F32)],
        compiler_params=_params("parallel", "parallel", "arbitrary"),
        name="retention",
    )(lg, gl, qkvg, qkvg, qkvg, qkvg, s0, g_out.reshape(1, RET_HEADS * dv))
    return o, s_out


def _softmax_weights(s):
    m = jnp.max(s, axis=-1, keepdims=True)
    p = jnp.exp(s - m)
    return p, jnp.sum(p, axis=-1, keepdims=True)


def _att_prompt_kernel(q_ref, k0_ref, k1_ref, k2_ref, v0_ref, v1_ref, v2_ref, bias_ref, o_ref):
    qb = pl.program_id(1)
    k_pos = qb * ATT_QB - BAND_PAST + lax.broadcasted_iota(jnp.int32, (1, ATT_KW), 1)
    pos_mask = jnp.where(k_pos >= 0, 0.0, MASK_VALUE).astype(F32)
    for hh in range(ATT_HP):
        cols = slice(hh * ATT_HD, (hh + 1) * ATT_HD)
        q = q_ref[:, cols]
        k = jnp.concatenate([k0_ref[:, cols], k1_ref[:, cols], k2_ref[:, cols]], axis=0)
        v = jnp.concatenate([v0_ref[:, cols], v1_ref[:, cols], v2_ref[:, cols]], axis=0)
        s = lax.dot_general(q, k, (((1,), (1,)), ((), ())), preferred_element_type=F32)
        s = s * (ATT_HD ** -0.5) + bias_ref[hh] + pos_mask
        p, l = _softmax_weights(s)
        o = jnp.dot(p.astype(BF16), v, preferred_element_type=F32)
        o_ref[:, cols] = (o / l).astype(o_ref.dtype)


def _att_prompt(qkv, bias, n_rows):
    nqb = n_rows // ATT_QB
    nw = ATT_KW // ATT_QB
    ng = ATT_HEADS // ATT_HP
    blk = (ATT_QB, ATT_HP * ATT_HD)

    def kv_spec(back, section):
        return pl.BlockSpec(blk, lambda g, i: (jnp.maximum(i - back, 0), g + section * ng))

    k_specs = [kv_spec(nw - 1 - n, 1) for n in range(nw)]
    v_specs = [kv_spec(nw - 1 - n, 2) for n in range(nw)]
    return pl.pallas_call(
        _att_prompt_kernel,
        grid=(ng, nqb),
        in_specs=[pl.BlockSpec(blk, lambda g, i: (i, g))] + k_specs + v_specs + [
            pl.BlockSpec((ATT_HP, ATT_QB, ATT_KW), lambda g, i: (g, 0, 0)),
        ],
        out_specs=pl.BlockSpec(blk, lambda g, i: (i, g)),
        out_shape=jax.ShapeDtypeStruct((n_rows, ATT_HEADS * ATT_HD), BF16),
        compiler_params=_params("parallel", "arbitrary"),
        name="att_prompt",
    )(*([qkv] * (1 + 2 * nw)), bias)


def _att_sample_kernel(q_ref, kn_ref, vn_ref, kc_ref, vc_ref, bc_ref, bn_ref, o_ref, kd_ref, vd_ref):
    scale = ATT_HD ** -0.5
    dims = (((1,), (1,)), ((), ()))
    lc = kc_ref.shape[0] // ATT_HEADS
    kd_ref[...] = pltpu.einshape("(th)d->htd", kc_ref[...], h=ATT_HEADS).astype(BF16)
    vd_ref[...] = pltpu.einshape("(th)d->htd", vc_ref[...], h=ATT_HEADS).astype(BF16)
    for h in range(ATT_HEADS):
        cols = slice(h * ATT_HD, (h + 1) * ATT_HD)
        q = q_ref[:, cols]
        kc = kd_ref[h]
        vc = vd_ref[h]
        sc = lax.dot_general(q, kc, dims, preferred_element_type=F32) * scale + bc_ref[h]
        sn = lax.dot_general(q, kn_ref[:, cols], dims, preferred_element_type=F32) * scale + bn_ref[h]
        m = jnp.maximum(jnp.max(sc, axis=-1, keepdims=True), jnp.max(sn, axis=-1, keepdims=True))
        pc = jnp.exp(sc - m)
        pn = jnp.exp(sn - m)
        l = jnp.sum(pc, axis=-1, keepdims=True) + jnp.sum(pn, axis=-1, keepdims=True)
        o = jnp.dot(pc.astype(BF16), vc, preferred_element_type=F32)
        o = o + jnp.dot(pn.astype(BF16), vn_ref[:, cols], preferred_element_type=F32)
        o_ref[:, cols] = (o / l).astype(o_ref.dtype)


def _att_sample(qkv, row0, cache_k, cache_v, layer, bias_c, bias_n):
    _, nb, lc, nh, hd = cache_k.shape
    d = nh * hd
    ts = (qkv.shape[0] - row0) // nb
    rb0 = row0 // ts
    new_specs = [pl.BlockSpec((ts, d), lambda b, section=section: (b + rb0, section)) for section in range(3)]
    cache_k = cache_k.reshape(cache_k.shape[0], nb, lc * nh, hd)
    cache_v = cache_v.reshape(cache_v.shape[0], nb, lc * nh, hd)
    cache_spec = pl.BlockSpec((None, None, lc * nh, hd), lambda b: (layer, b, 0, 0))
    return pl.pallas_call(
        _att_sample_kernel,
        grid=(nb,),
        in_specs=new_specs + [cache_spec, cache_spec,
                              pl.BlockSpec((nh, ts, lc), lambda b: (0, 0, 0)),
                              pl.BlockSpec((nh, ts, ts), lambda b: (0, 0, 0))],
        out_specs=pl.BlockSpec((ts, d), lambda b: (b, 0)),
        out_shape=jax.ShapeDtypeStruct((nb * ts, d), BF16),
        scratch_shapes=[pltpu.VMEM((nh, lc, hd), BF16), pltpu.VMEM((nh, lc, hd), BF16)],
        compiler_params=_params("parallel"),
        name="att_sample",
    )(qkv, qkv, qkv, cache_k, cache_v, bias_c, bias_n)


def _band_bias(rel_table, nq, nk, chunk_band):
    nh = rel_table.shape[0]
    n = nq + nk - 1
    q_minus_k = nq - 1 - jnp.arange(n, dtype=jnp.int32) + BAND_PAST
    r = rel_table[:, jnp.clip(q_minus_k, -REL_CLIP, REL_CLIP) + REL_CLIP].astype(F32)
    r = jnp.pad(r, ((0, 0), (0, 1)))
    skew = jnp.tile(r, (1, nq))[:, :nq * n].reshape(nh, nq, n)
    bias = skew[:, :, nq - 1:nq - 1 + nk]
    if chunk_band:
        qc = jnp.arange(nq, dtype=jnp.int32)[:, None] // CHUNK
        kc = jnp.arange(nk, dtype=jnp.int32)[None, :] // CHUNK
        visible = (kc >= qc) & (kc <= qc + N_PREV_CHUNKS)
        bias = jnp.where(visible[None], bias, MASK_VALUE)
    return bias


def kernel(x_prompt, x_sample, cache_conv, state_ret, cache_att_k, cache_att_v, g_ffn1, w_ffn1_gu, w_ffn1_down, g_mix, g_ffn2, w_ffn2_gu, w_ffn2_down, w_conv_in, w_conv_dw, b_conv_dw, g_conv_norm, w_conv_out, w_ret_in, g_ret_out, w_ret_out, w_att_qkv, g_att_q, g_att_k, att_rel_bias, w_att_out):
    bp, tp, d = x_prompt.shape
    bs, ts, _ = x_sample.shape
    depth = g_mix.shape[0]
    n_p = bp * tp
    n_s = bs * ts
    assert bp == 1, "the prompt path carries conv / retention state along one sequence"

    ffn_stacks = ((w_ffn1_gu, w_ffn1_down), (w_ffn2_gu, w_ffn2_down))
    ffn_w = (w_ffn1_gu[0].astype(BF16), w_ffn1_down[0].astype(BF16))
    w_ret_out = w_ret_out.astype(BF16)
    w_conv_in = w_conv_in.astype(BF16)

    pos = jnp.concatenate([jnp.arange(tp, dtype=jnp.int32),
                           jnp.tile(PAST_LEN + jnp.arange(ts, dtype=jnp.int32), bs)])
    half = RET_DK // 2
    freq = ROPE_BASE ** (-jnp.arange(half, dtype=F32) / half)
    ang = pos.astype(F32)[:, None] * freq[None, :]
    rope = (jnp.cos(ang), jnp.sin(ang))
    log_g = jnp.log1p(-jnp.exp2(-5.0 - jnp.arange(RET_HEADS, dtype=F32)))
    l_p = RET_L if tp % RET_L == 0 else tp
    l_s = ts

    conv_p, conv_s, ret_p, ret_s = [], [], [], []
    kp_l, vp_l, ks_l, vs_l = [], [], [], []
    xs = [x_prompt.reshape(n_p, d), x_sample.reshape(n_s, d)]
    for i in range(depth):
        kind, j = i % N_MIXERS, i // N_MIXERS
        x, *ffn_w = _ffn(xs, g_ffn1[i], *ffn_w, n_p, next_weights=ffn_stacks[1] + (i,))
        if kind == 0:
            u_p, z_p = _glu_conv_prompt(x, g_mix[i], w_conv_in, j, n_p, w_conv_dw, b_conv_dw[j], g_conv_norm[j])
            u_s = _glu(x, g_mix[i], w_conv_in, j, n_p)
            z_s = _conv_sample(u_s, 0, cache_conv, w_conv_dw, b_conv_dw[j], g_conv_norm[j], j)
            x = _out_proj(x, z_p, z_s, w_conv_out, j)
            conv_p.append(u_p[n_p - CONV_HALO:].reshape(bp, CONV_HALO, d))
            conv_s.append(u_s.reshape(bs, ts, d)[:, ts - CONV_HALO:])
        elif kind == 1:
            nq_t = RET_HEADS * RET_DK // PROJ_TN
            nv_t = RET_HEADS * RET_DV // PROJ_TN
            segments = [(nq_t, "rope_q", 0, False), (nq_t, "rope_k", 0, False),
                        (nv_t, "plain", 0, False), (nv_t, "silu", 0, False)]
            (qkvg,) = _proj(x, g_mix[i], w_ret_in, j, segments, rope=rope)
            zero_state = jnp.zeros((1, bp, RET_HEADS, RET_DK, RET_DV), F32)
            o_p, s_p = _retention(qkvg, zero_state, 0, g_ret_out[j], 0, bp, tp, l_p,
                                  (log_g, jnp.exp(log_g * l_p)))
            o_s, s_s = _retention(qkvg, state_ret, j, g_ret_out[j], n_p, bs, ts, l_s,
                                  (log_g, jnp.exp(log_g * l_s)))
            x = _out_proj(x, o_p, o_s, w_ret_out, j)
            ret_p.append(s_p)
            ret_s.append(s_s)
        else:
            nd_t = d // PROJ_TN
            segments = [(nd_t, "headnorm", 0, False), (nd_t, "headnorm", 1, True), (nd_t, "plain", 0, True)]
            qkv, kv32 = _proj(x, g_mix[i], w_att_qkv, j, segments, gains=jnp.stack([g_att_q[j], g_att_k[j]]))
            k32, v32 = kv32[:, :d], kv32[:, d:]
            lc = cache_att_k.shape[2]
            o_p = _att_prompt(qkv, _band_bias(att_rel_bias[j], ATT_QB, ATT_KW, True), n_p)
            bias_s = _band_bias(att_rel_bias[j], ts, lc + ts, False)
            o_s = _att_sample(qkv, n_p, cache_att_k, cache_att_v, j, bias_s[:, :, :lc], bias_s[:, :, lc:])
            x = _out_proj(x, o_p, o_s, w_att_out, j)
            keep = min(BAND_PAST, tp)
            kp_l.append(k32[n_p - keep:n_p].reshape(bp, keep, ATT_HEADS, ATT_HD))
            vp_l.append(v32[n_p - keep:n_p].reshape(bp, keep, ATT_HEADS, ATT_HD))
            ks_l.append(k32[n_p:].reshape(bs, ts, ATT_HEADS, ATT_HD))
            vs_l.append(v32[n_p:].reshape(bs, ts, ATT_HEADS, ATT_HD))
        last = i == depth - 1
        xs = _ffn([x], g_ffn2[i], *ffn_w, n_p, split_out=last,
                  next_weights=None if last else ffn_stacks[0] + (i + 1,))
        if not last:
            xs, ffn_w = xs[:1], xs[1:]

    return (xs[0].reshape(bp, tp, d), xs[1].reshape(bs, ts, d),
            jnp.stack(conv_p), jnp.stack(conv_s),
            jnp.stack(ret_p), jnp.stack(ret_s),
            jnp.stack(kp_l), jnp.stack(vp_l),
            jnp.stack(ks_l), jnp.stack(vs_l))
```

```python
import functools

import jax
import jax.numpy as jnp
from jax import lax
from jax.experimental import pallas as pl
from jax.experimental.pallas import tpu as pltpu

F32 = jnp.float32
BF16 = jnp.bfloat16

CHUNK = 64
NORM_EPS = 1e-6
CONV_WIDTH = 31
CONV_HALO = CONV_WIDTH - 1
RET_HEADS = 8
RET_DK = 256
RET_DV = 512
ROPE_BASE = 10000.0
ATT_HEADS = 16
ATT_HD = 128
N_PREV_CHUNKS = 8
BAND_PAST = N_PREV_CHUNKS * CHUNK
REL_CLIP = 256
PAST_LEN = 4096
N_MIXERS = 3

LANES = 128
SUBLANES = 8
VMEM_BYTES = 64 * 1024 * 1024
VMEM_LIMIT_BYTES = 56 * 1024 * 1024
FFN_VMEM_LIMIT_BYTES = 60 * 1024 * 1024
assert FFN_VMEM_LIMIT_BYTES < VMEM_BYTES
MASK_VALUE = -1e30

FFN_TM = 1024
FFN_TM_SPLIT = 512
FFN_TF = 512
FFN_CAST_BLOCKS = 8
PROJ_TM = 1024
PROJ_TN = 1024
PROJ_SUB = 512
GLU_TN = 512
GLU_SUB = 256
OUT_TM = 512
CONV_TT = 256
CONV_ROWS = 64
CONV_COLS = 128
CONV_PAD = 32
RET_L = 256
RET_HP = 8
ATT_QB = 256
ATT_KW = ATT_QB + BAND_PAST
ATT_HP = 16


def _params(*sem, vmem_limit_bytes=VMEM_LIMIT_BYTES):
    return pltpu.CompilerParams(dimension_semantics=sem, vmem_limit_bytes=vmem_limit_bytes)


def _as_bf16(w):
    return w if w.dtype == BF16 else w.astype(BF16)


def _rmsnorm_rows(x, g):
    ms = jnp.mean(x * x, axis=-1, keepdims=True)
    return x * lax.rsqrt(ms + NORM_EPS) * g


def _silu(x):
    return x * jax.nn.sigmoid(x)


def _row_tile_specs(tm, width, n_pt):
    whole = pl.BlockSpec((tm, width), lambda i, j: (i, 0))
    prompt = pl.BlockSpec((tm, width), lambda i, j: (jnp.minimum(i, n_pt - 1), 0))
    sample = pl.BlockSpec((tm, width), lambda i, j: (jnp.maximum(i - n_pt, 0), 0))
    return whole, prompt, sample


def _ffn_kernel(*refs, n_pt, split_in, split_out, cast_next):
    refs = list(refs)
    x_refs = [refs.pop(0) for _ in range(2 if split_in else 1)]
    g_ref, wg_ref, wu_ref, wd_ref = refs[:4]
    refs = refs[4:]
    h_ref = refs.pop()
    if cast_next:
        next_gu_ref, next_down_ref = refs[:2]
        cast_gu_ref, cast_down_ref = refs[-2:]
        refs = refs[2:-2]
    o_refs = refs
    i = pl.program_id(0)
    j = pl.program_id(1)

    def body(x_ref, o_ref):
        @pl.when(j == 0)
        def _():
            x = x_ref[...]
            h_ref[...] = _rmsnorm_rows(x, g_ref[...]).astype(BF16)
            o_ref[...] = x

        h = h_ref[...]
        gate = jnp.dot(h, wg_ref[...], preferred_element_type=F32)
        up = jnp.dot(h, wu_ref[...], preferred_element_type=F32)
        a = (_silu(gate) * (0.5 * up)).astype(BF16)
        o_ref[...] += jnp.dot(a, wd_ref[...], preferred_element_type=F32)

    if not (split_in or split_out):
        body(x_refs[0], o_refs[0])
    else:
        pl.when(i < n_pt)(lambda: body(x_refs[0], o_refs[0]))
        pl.when(i >= n_pt)(lambda: body(x_refs[-1], o_refs[-1]))

    if cast_next:
        @pl.when(i < FFN_CAST_BLOCKS)
        def _():
            cast_gu_ref[...] = next_gu_ref[...].astype(BF16)
            cast_down_ref[...] = next_down_ref[...].astype(BF16)


def _ffn(xs, g, w_gu, w_down, n_p, split_out=False, next_weights=None):
    split_in = len(xs) == 2
    d = xs[0].shape[1]
    t = sum(x.shape[0] for x in xs)
    f = w_down.shape[0]
    nf = f // FFN_TF
    tm = FFN_TM_SPLIT if split_out else FFN_TM
    whole, prompt, sample = _row_tile_specs(tm, d, n_p // tm)
    n_pt = n_p // tm
    in_prompt = pl.BlockSpec((tm, d), lambda i, j: (jnp.minimum(i, n_pt - 1), 0), pipeline_mode=pl.Buffered(1))
    in_sample = pl.BlockSpec((tm, d), lambda i, j: (jnp.maximum(i - n_pt, 0), 0), pipeline_mode=pl.Buffered(1))
    in_specs = ([in_prompt, in_sample] if split_in else [whole]) + [
        pl.BlockSpec((1, d), lambda i, j: (0, 0)),
        pl.BlockSpec((d, FFN_TF), lambda i, j: (0, j)),
        pl.BlockSpec((d, FFN_TF), lambda i, j: (0, j + nf)),
        pl.BlockSpec((FFN_TF, d), lambda i, j: (j, 0)),
    ]
    args = list(xs) + [g.reshape(1, d), w_gu, w_gu, w_down]
    out_specs = [prompt, sample] if split_out else [whole]
    out_shape = ([jax.ShapeDtypeStruct((n_p, d), F32), jax.ShapeDtypeStruct((t - n_p, d), F32)] if split_out
                 else [jax.ShapeDtypeStruct((t, d), F32)])
    if next_weights is not None:
        gu_stack, down_stack, layer = next_weights
        nb = FFN_CAST_BLOCKS
        assert t // tm >= nb
        bi = lambda i, j: jnp.where(i < nb, i, nb - 1)
        bj = lambda i, j: jnp.where(i < nb, j, nf - 1)
        gu_blk, down_blk = (d // nb, 2 * f // nf), (f // nf, d // nb)
        in_specs += [pl.BlockSpec((None,) + gu_blk, lambda i, j: (layer, bi(i, j), bj(i, j))),
                     pl.BlockSpec((None,) + down_blk, lambda i, j: (layer, bj(i, j), bi(i, j)))]
        args += [gu_stack, down_stack]
        out_specs += [pl.BlockSpec(gu_blk, lambda i, j: (bi(i, j), bj(i, j))),
                      pl.BlockSpec(down_blk, lambda i, j: (bj(i, j), bi(i, j)))]
        out_shape += [jax.ShapeDtypeStruct((d, 2 * f), BF16), jax.ShapeDtypeStruct((f, d), BF16)]
    kern = functools.partial(_ffn_kernel, n_pt=n_p // tm, split_in=split_in, split_out=split_out,
                             cast_next=next_weights is not None)
    return pl.pallas_call(
        kern,
        grid=(t // tm, nf),
        in_specs=in_specs,
        out_specs=out_specs,
        out_shape=out_shape,
        scratch_shapes=[pltpu.VMEM((tm, d), BF16)],
        compiler_params=_params("arbitrary", "arbitrary", vmem_limit_bytes=FFN_VMEM_LIMIT_BYTES),
        name="ffn",
    )(*args)


def _proj_kernel(*refs, segments, has_rope, has_gain, has_f32):
    it = iter(refs)
    x_ref, g_ref, w_ref = next(it), next(it), next(it)
    cos_ref = sin_ref = gh_ref = o32_ref = None
    if has_rope:
        cos_ref, sin_ref = next(it), next(it)
    if has_gain:
        gh_ref = next(it)
    o_ref = next(it)
    if has_f32:
        o32_ref = next(it)
    h_ref = next(it)
    j = pl.program_id(1)

    @pl.when(j == 0)
    def _():
        h_ref[...] = _rmsnorm_rows(x_ref[...], g_ref[...]).astype(BF16)

    def epilogue(y, mode, gain_row):
        if mode == "silu":
            return _silu(y)
        if mode in ("rope_q", "rope_k"):
            cos, sin = cos_ref[...], sin_ref[...]
            scale = RET_DK ** -0.5 if mode == "rope_k" else 1.0
            half = RET_DK // 2
            parts = []
            for hh in range(y.shape[1] // RET_DK):
                x1 = y[:, hh * RET_DK: hh * RET_DK + half]
                x2 = y[:, hh * RET_DK + half: (hh + 1) * RET_DK]
                parts.append((x1 * cos - x2 * sin) * scale)
                parts.append((x1 * sin + x2 * cos) * scale)
            return jnp.concatenate(parts, axis=1)
        if mode == "headnorm":
            gh = gh_ref[gain_row:gain_row + 1, :]
            return jnp.concatenate([_rmsnorm_rows(y[:, hh * ATT_HD:(hh + 1) * ATT_HD], gh)
                                    for hh in range(y.shape[1] // ATT_HD)], axis=1)
        return y

    def column_tile(mode, gain_row, f32_copy):
        for c0 in range(0, w_ref.shape[1], PROJ_SUB):
            cols = slice(c0, c0 + PROJ_SUB)
            y = jnp.dot(h_ref[...], _as_bf16(w_ref[:, cols]), preferred_element_type=F32)
            z = epilogue(y, mode, gain_row)
            o_ref[:, cols] = z.astype(o_ref.dtype)
            if f32_copy:
                o32_ref[:, cols] = z

    j0 = 0
    for n_tiles, mode, gain_row, f32_copy in segments:
        in_segment = jnp.logical_and(j >= j0, j < j0 + n_tiles)
        pl.when(in_segment)(functools.partial(column_tile, mode, gain_row, f32_copy))
        j0 += n_tiles


def _proj(x, g, w, layer, segments, *, rope=None, gains=None):
    t, d = x.shape
    tm, tn = PROJ_TM, PROJ_TN
    n_tiles = sum(seg[0] for seg in segments)
    n32_tiles = sum(seg[0] for seg in segments if seg[3])
    j32 = n_tiles - n32_tiles
    in_specs = [
        pl.BlockSpec((tm, d), lambda i, j: (i, 0)),
        pl.BlockSpec((1, d), lambda i, j: (0, 0)),
        pl.BlockSpec((None, d, tn), lambda i, j: (layer, 0, j)),
    ]
    args = [x, g.reshape(1, d), w]
    if rope is not None:
        in_specs += [pl.BlockSpec((tm, RET_DK // 2), lambda i, j: (i, 0))] * 2
        args += list(rope)
    if gains is not None:
        in_specs.append(pl.BlockSpec(gains.shape, lambda i, j: (0, 0)))
        args.append(gains)
    out_specs = [pl.BlockSpec((tm, tn), lambda i, j: (i, j))]
    out_shape = [jax.ShapeDtypeStruct((t, n_tiles * tn), BF16)]
    if n32_tiles:
        out_specs.append(pl.BlockSpec((tm, tn), lambda i, j: (i, jnp.maximum(j - j32, 0))))
        out_shape.append(jax.ShapeDtypeStruct((t, n32_tiles * tn), F32))
    kern = functools.partial(_proj_kernel, segments=tuple(segments), has_rope=rope is not None,
                             has_gain=gains is not None, has_f32=bool(n32_tiles))
    return pl.pallas_call(
        kern,
        grid=(t // tm, n_tiles),
        in_specs=in_specs,
        out_specs=out_specs,
        out_shape=out_shape,
        scratch_shapes=[pltpu.VMEM((tm, d), BF16)],
        compiler_params=_params("arbitrary", "arbitrary"),
        name="proj",
    )(*args)


def _glu_kernel(x_ref, g_ref, wa_ref, wb_ref, o_ref, h_ref):
    @pl.when(pl.program_id(1) == 0)
    def _():
        h_ref[...] = _rmsnorm_rows(x_ref[...], g_ref[...]).astype(BF16)

    h = h_ref[...]
    a = jnp.dot(h, _as_bf16(wa_ref[...]), preferred_element_type=F32)
    b = jnp.dot(h, _as_bf16(wb_ref[...]), preferred_element_type=F32)
    o_ref[...] = a * jax.nn.sigmoid(b)


def _glu(x, g, w_in, layer, row0):
    d = x.shape[1]
    t = x.shape[0] - row0
    tm, tn = PROJ_TM, GLU_TN
    nb = d // tn
    rb0 = row0 // tm
    return pl.pallas_call(
        _glu_kernel,
        grid=(t // tm, nb),
        in_specs=[
            pl.BlockSpec((tm, d), lambda i, j: (i + rb0, 0)),
            pl.BlockSpec((1, d), lambda i, j: (0, 0)),
            pl.BlockSpec((None, d, tn), lambda i, j: (layer, 0, j)),
            pl.BlockSpec((None, d, tn), lambda i, j: (layer, 0, j + nb)),
        ],
        out_specs=pl.BlockSpec((tm, tn), lambda i, j: (i, j)),
        out_shape=jax.ShapeDtypeStruct((t, d), F32),
        scratch_shapes=[pltpu.VMEM((tm, d), BF16)],
        compiler_params=_params("parallel", "arbitrary"),
        name="conv_glu",
    )(x, g.reshape(1, d), w_in, w_in)


def _out_proj_kernel(x_ref, yp_ref, ys_ref, w_ref, o_ref, *scratch, n_pt):
    i = pl.program_id(0)
    if scratch:
        (wb_ref,) = scratch

        @pl.when(i == 0)
        def _():
            wb_ref[...] = w_ref[...].astype(BF16)
    else:
        wb_ref = w_ref

    def body(y_ref):
        for c0 in range(0, wb_ref.shape[1], PROJ_SUB):
            cols = slice(c0, c0 + PROJ_SUB)
            o_ref[:, cols] = x_ref[:, cols] + jnp.dot(y_ref[...], wb_ref[:, cols], preferred_element_type=F32)

    pl.when(i < n_pt)(lambda: body(yp_ref))
    pl.when(i >= n_pt)(lambda: body(ys_ref))


def _out_proj(x, y_p, y_s, w, layer):
    t, d = x.shape
    k = y_p.shape[1]
    tm = OUT_TM
    n_pt = y_p.shape[0] // tm
    row = lambda width: pl.BlockSpec((tm, width), lambda i: (i, 0))
    return pl.pallas_call(
        functools.partial(_out_proj_kernel, n_pt=n_pt),
        grid=(t // tm,),
        in_specs=[
            row(d),
            pl.BlockSpec((tm, k), lambda i: (jnp.minimum(i, n_pt - 1), 0)),
            pl.BlockSpec((tm, k), lambda i: (jnp.maximum(i - n_pt, 0), 0)),
            pl.BlockSpec((None, k, d), lambda i: (layer, 0, 0), pipeline_mode=pl.Buffered(1)),
        ],
        out_specs=row(d),
        out_shape=jax.ShapeDtypeStruct((t, d), F32),
        scratch_shapes=[] if w.dtype == BF16 else [pltpu.VMEM((k, d), BF16)],
        compiler_params=_params("arbitrary"),
        name="out_proj",
    )(x, y_p, y_s, w)


def _conv_taps(x, w_ref, cols, rows):
    off = CONV_PAD - CONV_HALO
    slab = x.shape[0]
    acc = jnp.zeros((rows, x.shape[1]), F32)
    for s in range(SUBLANES):
        xs = x if s == 0 else pltpu.roll(x, slab - s, 0)
        for a8 in range(0, CONV_PAD + SUBLANES, SUBLANES):
            w = a8 + s - off
            if 0 <= w < CONV_WIDTH:
                acc = acc + xs[a8:a8 + rows, :] * w_ref[w:w + 1, cols]
    return acc


def _glu_conv_kernel(x_ref, g_ref, w_ref, wdw_ref, b_ref, gn_ref, u_ref, z_ref, ext_ref, y_ref, *, tt):
    s = pl.program_id(0)
    d = x_ref.shape[-1]
    cur = s % 2
    prev = 1 - cur
    rows = CONV_ROWS
    slab = rows + CONV_PAD

    @pl.when(s == 0)
    def _():
        ext_ref[...] = jnp.zeros(ext_ref.shape, F32)

    h = _rmsnorm_rows(x_ref[...], g_ref[...]).astype(BF16)
    ext_ref[cur, 0:CONV_PAD, :] = ext_ref[prev, tt:tt + CONV_PAD, :]
    for c0 in range(0, d, GLU_SUB):
        cols = slice(c0, c0 + GLU_SUB)
        a = jnp.dot(h, w_ref[:, cols], preferred_element_type=F32)
        b = jnp.dot(h, w_ref[:, d + c0:d + c0 + GLU_SUB], preferred_element_type=F32)
        u = a * jax.nn.sigmoid(b)
        u_ref[:, cols] = u
        ext_ref[cur, CONV_PAD:CONV_PAD + tt, cols] = u

    for r0 in range(0, tt, rows):
        for c0 in range(0, d, CONV_COLS):
            cols = slice(c0, c0 + CONV_COLS)
            acc = _conv_taps(ext_ref[prev, r0:r0 + slab, cols], wdw_ref, cols, rows)
            y_ref[r0:r0 + rows, cols] = acc + b_ref[:, cols]
    z_ref[...] = _silu(_rmsnorm_rows(y_ref[...], gn_ref[...])).astype(z_ref.dtype)


def _glu_conv_prompt(x, g, w_in, layer, n_rows, w_dw, b_dw, g_norm):
    d = x.shape[1]
    tt = CONV_TT
    nt = n_rows // tt
    tile = lambda s: jnp.minimum(s, nt - 1)
    vec = pl.BlockSpec((1, d), lambda s: (0, 0))
    return pl.pallas_call(
        functools.partial(_glu_conv_kernel, tt=tt),
        grid=(nt + 1,),
        in_specs=[
            pl.BlockSpec((tt, d), lambda s: (tile(s), 0)),
            vec,
            pl.BlockSpec((None, d, 2 * d), lambda s: (layer, 0, 0), pipeline_mode=pl.Buffered(1)),
            pl.BlockSpec((None, CONV_WIDTH, d), lambda s: (layer, 0, 0)),
            vec,
            vec,
        ],
        out_specs=[pl.BlockSpec((tt, d), lambda s: (tile(s), 0)),
                   pl.BlockSpec((tt, d), lambda s: (jnp.maximum(s - 1, 0), 0))],
        out_shape=[jax.ShapeDtypeStruct((n_rows, d), F32), jax.ShapeDtypeStruct((n_rows, d), BF16)],
        scratch_shapes=[pltpu.VMEM((2, CONV_PAD + tt, d), F32), pltpu.VMEM((tt, d), F32)],
        compiler_params=_params("arbitrary"),
        name="glu_conv_prompt",
    )(x, g.reshape(1, d), w_in, w_dw, b_dw.reshape(1, d), g_norm.reshape(1, d))


def _conv_sample_kernel(u_ref, halo_ref, w_ref, b_ref, gn_ref, z_ref, ext_ref, *, tt):
    d = u_ref.shape[-1]
    off = CONV_PAD - CONV_HALO
    ext_ref[0:SUBLANES, :] = jnp.zeros((SUBLANES, d), F32)
    ext_ref[off:CONV_PAD, :] = halo_ref[...]
    ext_ref[CONV_PAD:CONV_PAD + tt, :] = u_ref[...]
    parts = []
    for c0 in range(0, d, CONV_COLS):
        cols = slice(c0, c0 + CONV_COLS)
        parts.append(_conv_taps(ext_ref[:, cols], w_ref, cols, tt) + b_ref[:, cols])
    y = jnp.concatenate(parts, axis=1)
    z_ref[...] = _silu(_rmsnorm_rows(y, gn_ref[...])).astype(z_ref.dtype)


def _conv_sample(u, row0, cache, w_dw, b_dw, g_norm, layer):
    _, nb, _, d = cache.shape
    tt = (u.shape[0] - row0) // nb
    rb0 = row0 // tt
    kern = functools.partial(_conv_sample_kernel, tt=tt)
    return pl.pallas_call(
        kern,
        grid=(nb,),
        in_specs=[
            pl.BlockSpec((tt, d), lambda b: (b + rb0, 0)),
            pl.BlockSpec((None, None, CONV_HALO, d), lambda b: (layer, b, 0, 0)),
            pl.BlockSpec((None, CONV_WIDTH, d), lambda b: (layer, 0, 0)),
            pl.BlockSpec((1, d), lambda b: (0, 0)),
            pl.BlockSpec((1, d), lambda b: (0, 0)),
        ],
        out_specs=pl.BlockSpec((tt, d), lambda b: (b, 0)),
        out_shape=jax.ShapeDtypeStruct((nb * tt, d), BF16),
        scratch_shapes=[pltpu.VMEM((CONV_PAD + tt, d), F32)],
        compiler_params=_params("parallel"),
        name="conv_sample",
    )(u, cache, w_dw, b_dw.reshape(1, d), g_norm.reshape(1, d))


def _ret_kernel(lg_ref, gl_ref, q_ref, k_ref, v_ref, sg_ref, s0_ref, go_ref, o_ref, sout_ref, s_ref, *, L):
    hg = pl.program_id(1)
    c = pl.program_id(2)

    @pl.when(c == 0)
    def _():
        s_ref[...] = s0_ref[...]

    ri = lax.broadcasted_iota(jnp.int32, (L, L), 0)
    ci = lax.broadcasted_iota(jnp.int32, (L, L), 1)
    diff = (ri - ci).astype(F32)
    row = lax.broadcasted_iota(jnp.int32, (L, 1), 0).astype(F32)
    for hh in range(RET_HP):
        lg = lg_ref[hg * RET_HP + hh]
        g_chunk = gl_ref[hg * RET_HP + hh]
        kcols = slice(hh * RET_DK, (hh + 1) * RET_DK)
        vcols = slice(hh * RET_DV, (hh + 1) * RET_DV)
        q, k, v = q_ref[:, kcols], k_ref[:, kcols], v_ref[:, vcols]
        decay = jnp.where(diff >= 0, jnp.exp(lg * jnp.maximum(diff, 0.0)), 0.0)
        inner = lax.dot_general(q, k, (((1,), (1,)), ((), ())), preferred_element_type=F32) * decay
        o = jnp.dot(inner.astype(BF16), v, preferred_element_type=F32)

        state = s_ref[hh]
        q_dec = jnp.exp(lg * (row + 1.0))
        o = o + jnp.dot(q, state.astype(BF16), preferred_element_type=F32) * q_dec
        k_dec = jnp.exp(lg * (L - 1.0 - row))
        kd = (k.astype(F32) * k_dec).astype(BF16)
        new_state = state * g_chunk + lax.dot_general(kd, v, (((0,), (0,)), ((), ())),
                                                      preferred_element_type=F32)
        s_ref[hh] = new_state

        o = _rmsnorm_rows(o, go_ref[:, vcols])
        o_ref[:, vcols] = (sg_ref[:, vcols].astype(F32) * o).astype(o_ref.dtype)

    @pl.when(c == pl.num_programs(2) - 1)
    def _():
        sout_ref[...] = s_ref[...]


def _retention(qkvg, s0, layer, g_out, row0, n_seq, seq_len, L, decay_consts):
    nc = seq_len // L
    rb0 = row0 // L
    hp, dk, dv = RET_HP, RET_DK, RET_DV
    ng = RET_HEADS // hp
    row_map = lambda b, g, c: b * nc + c + rb0
    kern = functools.partial(_ret_kernel, L=L)
    lg, gl = decay_consts
    v0 = 2 * ng * dk // dv
    o, s_out = pl.pallas_call(
        kern,
        grid=(n_seq, ng, nc),
        in_specs=[
            pl.BlockSpec(memory_space=pltpu.SMEM),
            pl.BlockSpec(memory_space=pltpu.SMEM),
            pl.BlockSpec((L, hp * dk), lambda b, g, c: (row_map(b, g, c), g)),
            pl.BlockSpec((L, hp * dk), lambda b, g, c: (row_map(b, g, c), g + ng)),
            pl.BlockSpec((L, hp * dv), lambda b, g, c: (row_map(b, g, c), g + v0)),
            pl.BlockSpec((L, hp * dv), lambda b, g, c: (row_map(b, g, c), g + v0 + ng)),
            pl.BlockSpec((None, None, hp, dk, dv), lambda b, g, c: (layer, b, g, 0, 0)),
            pl.BlockSpec((1, hp * dv), lambda b, g, c: (0, g)),
        ],
        out_specs=[
            pl.BlockSpec((L, hp * dv), lambda b, g, c: (b * nc + c, g)),
            pl.BlockSpec((None, hp, dk, dv), lambda b, g, c: (b, g, 0, 0)),
        ],
        out_shape=[
            jax.ShapeDtypeStruct((n_seq * seq_len, RET_HEADS * dv), BF16),
            jax.ShapeDtypeStruct((n_seq, RET_HEADS, dk, dv), F32),
        ],
        scratch_shapes=[pltpu.VMEM((hp, dk, dv), F32)],
        compiler_params=_params("parallel", "parallel", "arbitrary"),
        name="retention",
    )(lg, gl, qkvg, qkvg, qkvg, qkvg, s0, g_out.reshape(1, RET_HEADS * dv))
    return o, s_out


def _softmax_weights(s):
    m = jnp.max(s, axis=-1, keepdims=True)
    p = jnp.exp(s - m)
    return p, jnp.sum(p, axis=-1, keepdims=True)


def _att_prompt_kernel(q_ref, k0_ref, k1_ref, k2_ref, v0_ref, v1_ref, v2_ref, bias_ref, o_ref):
    qb = pl.program_id(1)
    k_pos = qb * ATT_QB - BAND_PAST + lax.broadcasted_iota(jnp.int32, (1, ATT_KW), 1)
    pos_mask = jnp.where(k_pos >= 0, 0.0, MASK_VALUE).astype(F32)
    for hh in range(ATT_HP):
        cols = slice(hh * ATT_HD, (hh + 1) * ATT_HD)
        q = q_ref[:, cols]
        k = jnp.concatenate([k0_ref[:, cols], k1_ref[:, cols], k2_ref[:, cols]], axis=0)
        v = jnp.concatenate([v0_ref[:, cols], v1_ref[:, cols], v2_ref[:, cols]], axis=0)
        s = lax.dot_general(q, k, (((1,), (1,)), ((), ())), preferred_element_type=F32)
        s = s * (ATT_HD ** -0.5) + bias_ref[hh] + pos_mask
        p, l = _softmax_weights(s)
        o = jnp.dot(p.astype(BF16), v, preferred_element_type=F32)
        o_ref[:, cols] = (o / l).astype(o_ref.dtype)


def _att_prompt(qkv, bias, n_rows):
    nqb = n_rows // ATT_QB
    nw = ATT_KW // ATT_QB
    ng = ATT_HEADS // ATT_HP
    blk = (ATT_QB, ATT_HP * ATT_HD)

    def kv_spec(back, section):
        return pl.BlockSpec(blk, lambda g, i: (jnp.maximum(i - back, 0), g + section * ng))

    k_specs = [kv_spec(nw - 1 - n, 1) for n in range(nw)]
    v_specs = [kv_spec(nw - 1 - n, 2) for n in range(nw)]
    return pl.pallas_call(
        _att_prompt_kernel,
        grid=(ng, nqb),
        in_specs=[pl.BlockSpec(blk, lambda g, i: (i, g))] + k_specs + v_specs + [
            pl.BlockSpec((ATT_HP, ATT_QB, ATT_KW), lambda g, i: (g, 0, 0)),
        ],
        out_specs=pl.BlockSpec(blk, lambda g, i: (i, g)),
        out_shape=jax.ShapeDtypeStruct((n_rows, ATT_HEADS * ATT_HD), BF16),
        compiler_params=_params("parallel", "arbitrary"),
        name="att_prompt",
    )(*([qkv] * (1 + 2 * nw)), bias)


def _att_sample_kernel(q_ref, kn_ref, vn_ref, kc_ref, vc_ref, bc_ref, bn_ref, o_ref, kd_ref, vd_ref):
    scale = ATT_HD ** -0.5
    dims = (((1,), (1,)), ((), ()))
    lc = kc_ref.shape[0] // ATT_HEADS
    kd_ref[...] = pltpu.einshape("(th)d->htd", kc_ref[...], h=ATT_HEADS).astype(BF16)
    vd_ref[...] = pltpu.einshape("(th)d->htd", vc_ref[...], h=ATT_HEADS).astype(BF16)
    for h in range(ATT_HEADS):
        cols = slice(h * ATT_HD, (h + 1) * ATT_HD)
        q = q_ref[:, cols]
        kc = kd_ref[h]
        vc = vd_ref[h]
        sc = lax.dot_general(q, kc, dims, preferred_element_type=F32) * scale + bc_ref[h]
        sn = lax.dot_general(q, kn_ref[:, cols], dims, preferred_element_type=F32) * scale + bn_ref[h]
        m = jnp.maximum(jnp.max(sc, axis=-1, keepdims=True), jnp.max(sn, axis=-1, keepdims=True))
        pc = jnp.exp(sc - m)
        pn = jnp.exp(sn - m)
        l = jnp.sum(pc, axis=-1, keepdims=True) + jnp.sum(pn, axis=-1, keepdims=True)
        o = jnp.dot(pc.astype(BF16), vc, preferred_element_type=F32)
        o = o + jnp.dot(pn.astype(BF16), vn_ref[:, cols], preferred_element_type=F32)
        o_ref[:, cols] = (o / l).astype(o_ref.dtype)


def _att_sample(qkv, row0, cache_k, cache_v, layer, bias_c, bias_n):
    _, nb, lc, nh, hd = cache_k.shape
    d = nh * hd
    ts = (qkv.shape[0] - row0) // nb
    rb0 = row0 // ts
    new_specs = [pl.BlockSpec((ts, d), lambda b, section=section: (b + rb0, section)) for section in range(3)]
    cache_k = cache_k.reshape(cache_k.shape[0], nb, lc * nh, hd)
    cache_v = cache_v.reshape(cache_v.shape[0], nb, lc * nh, hd)
    cache_spec = pl.BlockSpec((None, None, lc * nh, hd), lambda b: (layer, b, 0, 0))
    return pl.pallas_call(
        _att_sample_kernel,
        grid=(nb,),
        in_specs=new_specs + [cache_spec, cache_spec,
                              pl.BlockSpec((nh, ts, lc), lambda b: (0, 0, 0)),
                              pl.BlockSpec((nh, ts, ts), lambda b: (0, 0, 0))],
        out_specs=pl.BlockSpec((ts, d), lambda b: (b, 0)),
        out_shape=jax.ShapeDtypeStruct((nb * ts, d), BF16),
        scratch_shapes=[pltpu.VMEM((nh, lc, hd), BF16), pltpu.VMEM((nh, lc, hd), BF16)],
        compiler_params=_params("parallel"),
        name="att_sample",
    )(qkv, qkv, qkv, cache_k, cache_v, bias_c, bias_n)


def _band_bias(rel_table, nq, nk, chunk_band):
    nh = rel_table.shape[0]
    n = nq + nk - 1
    q_minus_k = nq - 1 - jnp.arange(n, dtype=jnp.int32) + BAND_PAST
    r = rel_table[:, jnp.clip(q_minus_k, -REL_CLIP, REL_CLIP) + REL_CLIP].astype(F32)
    r = jnp.pad(r, ((0, 0), (0, 1)))
    skew = jnp.tile(r, (1, nq))[:, :nq * n].reshape(nh, nq, n)
    bias = skew[:, :, nq - 1:nq - 1 + nk]
    if chunk_band:
        qc = jnp.arange(nq, dtype=jnp.int32)[:, None] // CHUNK
        kc = jnp.arange(nk, dtype=jnp.int32)[None, :] // CHUNK
        visible = (kc >= qc) & (kc <= qc + N_PREV_CHUNKS)
        bias = jnp.where(visible[None], bias, MASK_VALUE)
    return bias


def kernel(x_prompt, x_sample, cache_conv, state_ret, cache_att_k, cache_att_v, g_ffn1, w_ffn1_gu, w_ffn1_down, g_mix, g_ffn2, w_ffn2_gu, w_ffn2_down, w_conv_in, w_conv_dw, b_conv_dw, g_conv_norm, w_conv_out, w_ret_in, g_ret_out, w_ret_out, w_att_qkv, g_att_q, g_att_k, att_rel_bias, w_att_out):
    bp, tp, d = x_prompt.shape
    bs, ts, _ = x_sample.shape
    depth = g_mix.shape[0]
    n_p = bp * tp
    n_s = bs * ts
    assert bp == 1, "the prompt path carries conv / retention state along one sequence"

    ffn_stacks = ((w_ffn1_gu, w_ffn1_down), (w_ffn2_gu, w_ffn2_down))
    ffn_w = (w_ffn1_gu[0].astype(BF16), w_ffn1_down[0].astype(BF16))
    w_ret_out = w_ret_out.astype(BF16)
    w_conv_in = w_conv_in.astype(BF16)

    pos = jnp.concatenate([jnp.arange(tp, dtype=jnp.int32),
                           jnp.tile(PAST_LEN + jnp.arange(ts, dtype=jnp.int32), bs)])
    half = RET_DK // 2
    freq = ROPE_BASE ** (-jnp.arange(half, dtype=F32) / half)
    ang = pos.astype(F32)[:, None] * freq[None, :]
    rope = (jnp.cos(ang), jnp.sin(ang))
    log_g = jnp.log1p(-jnp.exp2(-5.0 - jnp.arange(RET_HEADS, dtype=F32)))
    l_p = RET_L if tp % RET_L == 0 else tp
    l_s = ts

    conv_p, conv_s, ret_p, ret_s = [], [], [], []
    kp_l, vp_l, ks_l, vs_l = [], [], [], []
    xs = [x_prompt.reshape(n_p, d), x_sample.reshape(n_s, d)]
    for i in range(depth):
        kind, j = i % N_MIXERS, i // N_MIXERS
        x, *ffn_w = _ffn(xs, g_ffn1[i], *ffn_w, n_p, next_weights=ffn_stacks[1] + (i,))
        if kind == 0:
            u_p, z_p = _glu_conv_prompt(x, g_mix[i], w_conv_in, j, n_p, w_conv_dw, b_conv_dw[j], g_conv_norm[j])
            u_s = _glu(x, g_mix[i], w_conv_in, j, n_p)
            z_s = _conv_sample(u_s, 0, cache_conv, w_conv_dw, b_conv_dw[j], g_conv_norm[j], j)
            x = _out_proj(x, z_p, z_s, w_conv_out, j)
            conv_p.append(u_p[n_p - CONV_HALO:].reshape(bp, CONV_HALO, d))
            conv_s.append(u_s.reshape(bs, ts, d)[:, ts - CONV_HALO:])
        elif kind == 1:
            nq_t = RET_HEADS * RET_DK // PROJ_TN
            nv_t = RET_HEADS * RET_DV // PROJ_TN
            segments = [(nq_t, "rope_q", 0, False), (nq_t, "rope_k", 0, False),
                        (nv_t, "plain", 0, False), (nv_t, "silu", 0, False)]
            (qkvg,) = _proj(x, g_mix[i], w_ret_in, j, segments, rope=rope)
            zero_state = jnp.zeros((1, bp, RET_HEADS, RET_DK, RET_DV), F32)
            o_p, s_p = _retention(qkvg, zero_state, 0, g_ret_out[j], 0, bp, tp, l_p,
                                  (log_g, jnp.exp(log_g * l_p)))
            o_s, s_s = _retention(qkvg, state_ret, j, g_ret_out[j], n_p, bs, ts, l_s,
                                  (log_g, jnp.exp(log_g * l_s)))
            x = _out_proj(x, o_p, o_s, w_ret_out, j)
            ret_p.append(s_p)
            ret_s.append(s_s)
        else:
            nd_t = d // PROJ_TN
            segments = [(nd_t, "headnorm", 0, False), (nd_t, "headnorm", 1, True), (nd_t, "plain", 0, True)]
            qkv, kv32 = _proj(x, g_mix[i], w_att_qkv, j, segments, gains=jnp.stack([g_att_q[j], g_att_k[j]]))
            k32, v32 = kv32[:, :d], kv32[:, d:]
            lc = cache_att_k.shape[2]
            o_p = _att_prompt(qkv, _band_bias(att_rel_bias[j], ATT_QB, ATT_KW, True), n_p)
            bias_s = _band_bias(att_rel_bias[j], ts, lc + ts, False)
            o_s = _att_sample(qkv, n_p, cache_att_k, cache_att_v, j, bias_s[:, :, :lc], bias_s[:, :, lc:])
            x = _out_proj(x, o_p, o_s, w_att_out, j)
            keep = min(BAND_PAST, tp)
            kp_l.append(k32[n_p - keep:n_p].reshape(bp, keep, ATT_HEADS, ATT_HD))
            vp_l.append(v32[n_p - keep:n_p].reshape(bp, keep, ATT_HEADS, ATT_HD))
            ks_l.append(k32[n_p:].reshape(bs, ts, ATT_HEADS, ATT_HD))
            vs_l.append(v32[n_p:].reshape(bs, ts, ATT_HEADS, ATT_HD))
        last = i == depth - 1
        xs = _ffn([x], g_ffn2[i], *ffn_w, n_p, split_out=last,
                  next_weights=None if last else ffn_stacks[0] + (i + 1,))
        if not last:
            xs, ffn_w = xs[:1], xs[1:]

    return (xs[0].reshape(bp, tp, d), xs[1].reshape(bs, ts, d),
            jnp.stack(conv_p), jnp.stack(conv_s),
            jnp.stack(ret_p), jnp.stack(ret_s),
            jnp.stack(kp_l), jnp.stack(vp_l),
            jnp.stack(ks_l), jnp.stack(vs_l))
```
